```python
import math
import jax, jax.numpy as jnp
from jax import lax
import numpy as np

D_MODEL = 1024
BATCH = 8
SEQ = 4096
DEPTH = 4

N_HEADS = 4
V_WIDTH = D_MODEL
HEAD_V = V_WIDTH // N_HEADS
HEAD_K = HEAD_V // 2
QK_WIDTH = N_HEADS * HEAD_K
N_BRANCH = 3
CHUNK = 64
CONV_K = 4
ROPE_BASE = 10000.0
RET_DECAY_OFFSET = 5.0
N_EXPERTS = 16
N_GROUPS = 4
EXPERTS_PER_GROUP = N_EXPERTS // N_GROUPS
TOP_K = 2
D_EXPERT = D_MODEL // 2
ALPHA = (2 * DEPTH) ** 0.25
BETA = (8 * DEPTH) ** -0.25
LN_EPS = 1e-5
NORM_EPS = 1e-6
IN_WIDTHS = (
    QK_WIDTH, QK_WIDTH, V_WIDTH, V_WIDTH,
    2 * QK_WIDTH + V_WIDTH, V_WIDTH, N_HEADS, N_HEADS,
    QK_WIDTH, QK_WIDTH, V_WIDTH, V_WIDTH,
    N_BRANCH * D_MODEL,
)
N_IN = sum(IN_WIDTHS)

kernel_name = 'hybrid_retention_gdn_hgrn2_grouped_moe'


def _split_columns(t, widths):
    offs = np.cumsum(widths)[:-1].tolist()
    return jnp.split(t, offs, axis=-1)


def _heads(t, d):
    return t.reshape(t.shape[0], t.shape[1], -1, d)


def _to_chunks(t):
    b, s, h, e = t.shape
    return t.reshape(b, s // CHUNK, CHUNK, h, e).transpose(1, 0, 3, 2, 4)


def _from_chunks(t):
    n, b, h, c, e = t.shape
    return t.transpose(1, 0, 3, 2, 4).reshape(b, n * c, h, e)


def _layer_norm(x, g, b):
    xf = x.astype(jnp.float32)
    mu = jnp.mean(xf, axis=-1, keepdims=True)
    var = jnp.mean(jnp.square(xf - mu), axis=-1, keepdims=True)
    y = (xf - mu) * lax.rsqrt(var + LN_EPS) * g.astype(jnp.float32) + b.astype(jnp.float32)
    return y.astype(x.dtype)


def _head_groupnorm(o):
    mu = jnp.mean(o, axis=-1, keepdims=True)
    var = jnp.mean(jnp.square(o - mu), axis=-1, keepdims=True)
    return (o - mu) * lax.rsqrt(var + NORM_EPS)


def _head_rms(o, g):
    return o * lax.rsqrt(jnp.mean(jnp.square(o), axis=-1, keepdims=True) + NORM_EPS) * g.astype(jnp.float32)


def _l2norm(t):
    return t * lax.rsqrt(jnp.sum(jnp.square(t), axis=-1, keepdims=True) + NORM_EPS)


def _rotary(t, positions):
    half = HEAD_K // 2
    inv_freq = 1.0 / (ROPE_BASE ** jnp.linspace(0.0, 1.0, half, dtype=jnp.float32))
    ang = positions.astype(jnp.float32)[..., None, None] * inv_freq
    cos, sin = jnp.cos(ang), jnp.sin(ang)
    t1, t2 = t[..., :half], t[..., half:]
    return jnp.concatenate([t1 * cos - t2 * sin, t1 * sin + t2 * cos], axis=-1)


def _causal_conv(t, w):
    c = t.shape[-1]
    return lax.conv_general_dilated(t, w[:, None, :], window_strides=(1,),
                                    padding=((CONV_K - 1, 0),),
                                    dimension_numbers=('NWC', 'WIO', 'NWC'),
                                    feature_group_count=c)


def _retention(q, k, v):
    lg = jnp.log1p(-(2.0 ** (-RET_DECAY_OFFSET - jnp.arange(N_HEADS, dtype=jnp.float32))))
    idx = jnp.arange(CHUNK, dtype=jnp.float32)
    rel = idx[:, None] - idx[None, :]
    intra = jnp.exp(jnp.where(rel >= 0, rel[None] * lg[:, None, None], -jnp.inf))
    q_dec = jnp.exp(lg[:, None] * (idx + 1.0))[..., None]
    k_dec = jnp.exp(lg[:, None] * (CHUNK - 1.0 - idx))[..., None]
    c_dec = jnp.exp(lg * CHUNK)[:, None, None]

    def step(state, xs):
        qc, kc, vc = xs
        scores = jnp.einsum('bhtk,bhsk->bhts', qc, kc) * intra
        out = (jnp.einsum('bhts,bhsv->bhtv', scores, vc)
               + jnp.einsum('bhtk,bhkv->bhtv', qc * q_dec, state))
        state = c_dec * state + jnp.einsum('bhsk,bhsv->bhkv', kc * k_dec, vc)
        return state, out

    s0 = jnp.zeros((q.shape[0], N_HEADS, HEAD_K, HEAD_V), jnp.float32)
    _, out = lax.scan(step, s0, (_to_chunks(q), _to_chunks(k), _to_chunks(v)))
    return _from_chunks(out)


def _gated_delta(q, k, v, g, beta):
    tri = jnp.tril(jnp.ones((CHUNK, CHUNK), bool))
    strict = jnp.tril(jnp.ones((CHUNK, CHUNK), bool), k=-1)
    gch = _to_chunks(g[..., None])[..., 0]
    bch = _to_chunks(beta[..., None])[..., 0]

    def step(state, xs):
        qc, kc, vc, gc, bc = xs
        gcum = jnp.cumsum(gc, axis=-1)
        decay = jnp.exp(jnp.where(tri, gcum[..., :, None] - gcum[..., None, :], -jnp.inf))
        kb = kc * bc[..., None]
        a = jnp.where(strict, jnp.einsum('bhtk,bhsk->bhts', kb, kc) * decay, 0.0)
        rhs = jnp.concatenate([vc * bc[..., None], kb * jnp.exp(gcum)[..., None]], axis=-1)
        sol = lax.linalg.triangular_solve(a, rhs, left_side=True, lower=True, unit_diagonal=True)
        u, w = sol[..., :HEAD_V], sol[..., HEAD_V:]
        v_new = u - jnp.einsum('bhtk,bhkv->bhtv', w, state)
        attn = jnp.einsum('bhtk,bhsk->bhts', qc, kc) * decay
        out = (jnp.einsum('bhtk,bhkv->bhtv', qc * jnp.exp(gcum)[..., None], state)
               + jnp.einsum('bhts,bhsv->bhtv', attn, v_new))
        g_last = gcum[..., -1]
        state = (jnp.exp(g_last)[..., None, None] * state
                 + jnp.einsum('bhsk,bhsv->bhkv', kc * jnp.exp(g_last[..., None] - gcum)[..., None], v_new))
        return state, out

    s0 = jnp.zeros((q.shape[0], N_HEADS, HEAD_K, HEAD_V), jnp.float32)
    _, out = lax.scan(step, s0, (_to_chunks(q), _to_chunks(k), _to_chunks(v), gch, bch))
    return _from_chunks(out)


def _hgrn2(q, k, v, log_f):
    tri = jnp.tril(jnp.ones((CHUNK, CHUNK), bool))[:, :, None]

    def step(state, xs):
        qc, kc, vc, lf = xs
        gcum = jnp.cumsum(lf, axis=2)
        diff = gcum[:, :, :, None, :] - gcum[:, :, None, :, :]
        decay = jnp.exp(jnp.where(tri, diff, -jnp.inf))
        attn = jnp.einsum('bhtk,bhsk,bhtsk->bhts', qc, kc, decay)
        out = (jnp.einsum('bhts,bhsv->bhtv', attn, vc)
               + jnp.einsum('bhtk,bhkv->bhtv', qc * jnp.exp(gcum), state))
        g_last = gcum[:, :, -1]
        state = (jnp.exp(g_last)[..., None] * state
                 + jnp.einsum('bhsk,bhsv->bhkv', kc * jnp.exp(g_last[:, :, None] - gcum), vc))
        return state, out

    s0 = jnp.zeros((q.shape[0], N_HEADS, HEAD_K, HEAD_V), jnp.float32)
    _, out = lax.scan(step, s0, (_to_chunks(q), _to_chunks(k), _to_chunks(v), _to_chunks(log_f)))
    return _from_chunks(out)


def _token_mixer(x, positions, w_in, conv_w, a_log, dt_bias, gdn_g, lb, hg_g, merge_b, w_branch, w_out):
    b, s, _ = x.shape
    f32 = jnp.float32
    proj = x @ w_in
    (r_q, r_k, r_v, r_g, g_qkv, g_z, g_b, g_a,
     h_q, h_f, h_i, h_g, m_g) = _split_columns(proj, IN_WIDTHS)

    q = _rotary(_heads(r_q.astype(f32), HEAD_K), positions)
    k = _rotary(_heads(r_k.astype(f32), HEAD_K), positions) * HEAD_K ** -0.5
    o = _head_groupnorm(_retention(q, k, _heads(r_v.astype(f32), HEAD_V)))
    y_ret = jax.nn.silu(r_g.astype(f32)) * o.reshape(b, s, V_WIDTH)

    qkv = jax.nn.silu(_causal_conv(g_qkv, conv_w)).astype(f32)
    q, k, v = _split_columns(qkv, (QK_WIDTH, QK_WIDTH, V_WIDTH))
    q = _l2norm(_heads(q, HEAD_K)) * HEAD_K ** -0.5
    k = _l2norm(_heads(k, HEAD_K))
    beta = jax.nn.sigmoid(g_b.astype(f32))
    g = -jnp.exp(a_log.astype(f32)) * jax.nn.softplus(g_a.astype(f32) + dt_bias.astype(f32))
    o = _head_rms(_gated_delta(q, k, _heads(v, HEAD_V), g, beta), gdn_g)
    y_gdn = jax.nn.silu(g_z.astype(f32)) * o.reshape(b, s, V_WIDTH)

    hf = h_f.astype(f32)
    lb32 = lb.astype(f32)
    log_f = jnp.logaddexp(jax.nn.log_sigmoid(hf), jnp.log(lb32) + jax.nn.log_sigmoid(-hf))
    k = (1.0 - lb32) * jax.nn.sigmoid(-hf)
    o = _hgrn2(_heads(h_q.astype(f32), HEAD_K) * HEAD_K ** -0.5, _heads(k, HEAD_K),
               _heads(h_i.astype(f32), HEAD_V), _heads(log_f, HEAD_K))
    o = _head_rms(o, hg_g)
    y_hg = jax.nn.sigmoid(h_g.astype(f32)) * o.reshape(b, s, V_WIDTH)

    branches = jnp.stack([y_ret, y_gdn, y_hg], axis=2).astype(x.dtype)
    projected = jnp.einsum('bsnv,nvd->bsnd', branches, w_branch)
    gates = jax.nn.sigmoid(m_g + merge_b).reshape(b, s, N_BRANCH, D_MODEL)
    merged = jnp.sum(gates * projected, axis=2)
    return merged @ w_out


def _moe(x, router_w, router_b, w_gate, w_up, w_down):
    b, s, d = x.shape
    t = x.reshape(b * s, d)
    scores = jax.nn.sigmoid(t.astype(jnp.float32) @ router_w.astype(jnp.float32))
    biased = (scores + router_b.astype(jnp.float32)).reshape(-1, N_GROUPS, EXPERTS_PER_GROUP)
    group_score = jnp.sum(lax.top_k(biased, TOP_K)[0], axis=-1)
    best = jnp.argmax(group_score, axis=-1)
    in_group = jnp.take_along_axis(biased, best[:, None, None], axis=1)[:, 0]
    _, local = lax.top_k(in_group, TOP_K)
    idx = best[:, None] * EXPERTS_PER_GROUP + local
    sel = jnp.take_along_axis(scores, idx, axis=-1)
    wts = sel / jnp.sum(sel, axis=-1, keepdims=True)
    combine = jnp.einsum('tk,tke->te', wts, jax.nn.one_hot(idx, N_EXPERTS, dtype=jnp.float32)).astype(x.dtype)
    y = jnp.zeros_like(t)
    for e in range(N_EXPERTS):
        h = jax.nn.silu(t @ w_gate[e]) * (t @ w_up[e])
        y = y + combine[:, e:e + 1] * (h @ w_down[e])
    return y.reshape(b, s, d)


def setup_inputs(seed: int = 0) -> dict:
    key = jax.random.key(seed)
    ks = jax.random.split(key, 24)
    f32 = jnp.float32

    def nrm(k, shape, scale):
        return scale * jax.random.normal(k, shape, f32)

    x = jax.random.normal(ks[0], (BATCH, SEQ, D_MODEL), f32)
    positions = jnp.tile(jnp.arange(SEQ, dtype=jnp.int32)[None, :], (BATCH, 1))
    w_in = nrm(ks[1], (DEPTH, D_MODEL, N_IN), D_MODEL ** -0.5)
    gdn_conv_w = nrm(ks[2], (DEPTH, CONV_K, 2 * QK_WIDTH + V_WIDTH), CONV_K ** -0.5)
    gdn_a_log = jnp.log(jax.random.uniform(ks[3], (DEPTH, N_HEADS), f32, 1.0, 16.0))
    dt = jnp.exp(jax.random.uniform(ks[4], (DEPTH, N_HEADS), f32, math.log(1e-3), math.log(1e-1)))
    gdn_dt_bias = dt + jnp.log(-jnp.expm1(-dt))
    gdn_norm_g = 1.0 + nrm(ks[5], (DEPTH, HEAD_V), 0.02)
    hgrn_lb = nrm(ks[6], (DEPTH, QK_WIDTH), 0.1)
    hgrn_norm_g = 1.0 + nrm(ks[7], (DEPTH, HEAD_V), 0.02)
    merge_b = nrm(ks[8], (DEPTH, N_BRANCH * D_MODEL), 0.02)
    w_branch = nrm(ks[9], (DEPTH, N_BRANCH, V_WIDTH, D_MODEL), BETA * V_WIDTH ** -0.5)
    w_out = nrm(ks[10], (DEPTH, D_MODEL, D_MODEL), BETA * D_MODEL ** -0.5)
    ln1_g = 1.0 + nrm(ks[11], (DEPTH, D_MODEL), 0.02)
    ln1_b = nrm(ks[12], (DEPTH, D_MODEL), 0.02)
    router_w = nrm(ks[13], (D_MODEL, N_EXPERTS), D_MODEL ** -0.5)
    router_b = nrm(ks[14], (N_EXPERTS,), 0.01)
    moe_w_gate = nrm(ks[15], (DEPTH, N_EXPERTS, D_MODEL, D_EXPERT), D_MODEL ** -0.5)
    moe_w_up = nrm(ks[16], (DEPTH, N_EXPERTS, D_MODEL, D_EXPERT), D_MODEL ** -0.5)
    moe_w_down = nrm(ks[17], (DEPTH, N_EXPERTS, D_EXPERT, D_MODEL), BETA * D_EXPERT ** -0.5)
    ln2_g = 1.0 + nrm(ks[18], (DEPTH, D_MODEL), 0.02)
    ln2_b = nrm(ks[19], (DEPTH, D_MODEL), 0.02)
    return {'x': x, 'positions': positions, 'w_in': w_in, 'gdn_conv_w': gdn_conv_w,
            'gdn_a_log': gdn_a_log, 'gdn_dt_bias': gdn_dt_bias, 'gdn_norm_g': gdn_norm_g,
            'hgrn_lb': hgrn_lb, 'hgrn_norm_g': hgrn_norm_g, 'merge_b': merge_b,
            'w_branch': w_branch, 'w_out': w_out, 'ln1_g': ln1_g, 'ln1_b': ln1_b,
            'router_w': router_w, 'router_b': router_b, 'moe_w_gate': moe_w_gate,
            'moe_w_up': moe_w_up, 'moe_w_down': moe_w_down, 'ln2_g': ln2_g, 'ln2_b': ln2_b}


def reference(x, positions, w_in, gdn_conv_w, gdn_a_log, gdn_dt_bias, gdn_norm_g, hgrn_lb,
              hgrn_norm_g, merge_b, w_branch, w_out, ln1_g, ln1_b, router_w, router_b,
              moe_w_gate, moe_w_up, moe_w_down, ln2_g, ln2_b):
    lb_all = jnp.cumsum(jax.nn.softmax(hgrn_lb.astype(jnp.float32), axis=0), axis=0)
    lb_all = lb_all - lb_all[0]
    for l in range(DEPTH):
        h = _token_mixer(x, positions, w_in[l], gdn_conv_w[l], gdn_a_log[l], gdn_dt_bias[l],
                         gdn_norm_g[l], lb_all[l], hgrn_norm_g[l], merge_b[l], w_branch[l], w_out[l])
        x = _layer_norm(ALPHA * x + h, ln1_g[l], ln1_b[l])
        h = _moe(x, router_w, router_b, moe_w_gate[l], moe_w_up[l], moe_w_down[l])
        x = _layer_norm(ALPHA * x + h, ln2_g[l], ln2_b[l])
    return x
```

```python
import functools
import math

import jax
import jax.numpy as jnp
from jax import lax
from jax.experimental import pallas as pl
from jax.experimental.pallas import tpu as pltpu

F32 = jnp.float32
BF16 = jnp.bfloat16

D_MODEL = 1024
DEPTH = 4
N_HEADS = 4
HEAD_V = 256
HEAD_K = 128
QK_WIDTH = N_HEADS * HEAD_K
V_WIDTH = N_HEADS * HEAD_V
N_BRANCH = 3
ROPE_BASE = 10000.0
RET_DECAY_OFFSET = 5.0
N_EXPERTS = 16
N_GROUPS = 4
EXPERTS_PER_GROUP = N_EXPERTS // N_GROUPS
D_EXPERT = D_MODEL // 2
ALPHA = (2 * DEPTH) ** 0.25
LN_EPS = 1e-5
NORM_EPS = 1e-6

SMALL_START = 2 * QK_WIDTH + 2 * V_WIDTH + (2 * QK_WIDTH + V_WIDTH) + V_WIDTH
SMALL_WIDTH = 2 * N_HEADS
N_MAIN = 12288
COL_RQ, COL_RK, COL_RV, COL_RG = 0, 512, 1024, 2048
COL_GQ, COL_GK, COL_GV, COL_GZ = 3072, 3584, 4096, 5120
COL_HQ, COL_HF, COL_HI, COL_HG = 6144, 6656, 7168, 8192
COL_MG = 9216

LANES = 128
SUBLANES = 8
VMEM_LIMIT = 56 * 1024 * 1024

RET_CHUNK = 256
GDN_CHUNK = 64
HGRN_CHUNK = 64
HGRN_SUB = 16


def _dot(a, b):
    return jnp.dot(a, b, preferred_element_type=F32)


def _dot_nt(a, b):
    return lax.dot_general(a, b, (((1,), (1,)), ((), ())), preferred_element_type=F32)


def _dot_tn(a, b):
    return lax.dot_general(a, b, (((0,), (0,)), ((), ())), preferred_element_type=F32)


def _cumsum_rows(x):
    n = x.shape[0]
    row = lax.broadcasted_iota(jnp.int32, x.shape, 0)
    s = 1
    while s < n:
        x = x + jnp.where(row >= s, pltpu.roll(x, s, axis=0), 0.0)
        s *= 2
    return x


def _layer_norm(z, g, b):
    mu = jnp.mean(z, axis=-1, keepdims=True)
    zc = z - mu
    var = jnp.mean(zc * zc, axis=-1, keepdims=True)
    return zc * lax.rsqrt(var + LN_EPS) * g + b


def _params(sem):
    return pltpu.CompilerParams(dimension_semantics=sem, vmem_limit_bytes=VMEM_LIMIT)


def _rope_kernel(pos_ref, invf_ref, cos_ref, sin_ref):
    ang = pos_ref[...].astype(F32) * invf_ref[...]
    lane = lax.broadcasted_iota(jnp.int32, ang.shape, 1)
    cos_ref[...] = jnp.cos(ang)
    s = jnp.sin(ang)
    sin_ref[...] = jnp.where(lane < HEAD_K // 2, -s, s)


def _rope_tables(positions):
    t = positions.size
    half = HEAD_K // 2
    inv_freq = 1.0 / (ROPE_BASE ** jnp.linspace(0.0, 1.0, half, dtype=F32))
    invf = jnp.concatenate([inv_freq, inv_freq])[None, :]
    ts = min(t, 2048)
    return pl.pallas_call(
        _rope_kernel,
        out_shape=(jax.ShapeDtypeStruct((t, HEAD_K), F32), jax.ShapeDtypeStruct((t, HEAD_K), F32)),
        grid=(t // ts,),
        in_specs=[pl.BlockSpec((ts, 1), lambda i: (i, 0)), pl.BlockSpec((1, HEAD_K), lambda i: (0, 0))],
        out_specs=(pl.BlockSpec((ts, HEAD_K), lambda i: (i, 0)), pl.BlockSpec((ts, HEAD_K), lambda i: (i, 0))),
        compiler_params=_params(("arbitrary",)),
        name="rope_tables",
    )(positions.reshape(t, 1), invf)


def _matmul_kernel(x_ref, w_ref, o_ref):
    o_ref[...] = _dot(x_ref[...], w_ref[...]).astype(o_ref.dtype)


def _in_proj(xb, w, out_dtype, tm, tn):
    t, d = xb.shape
    n = w.shape[1]
    tm = min(tm, t)
    tn = min(tn, n)
    return pl.pallas_call(
        _matmul_kernel,
        out_shape=jax.ShapeDtypeStruct((t, n), out_dtype),
        grid=(t // tm, n // tn),
        in_specs=[pl.BlockSpec((tm, d), lambda i, j: (i, 0)), pl.BlockSpec((d, tn), lambda i, j: (0, j))],
        out_specs=pl.BlockSpec((tm, tn), lambda i, j: (i, j)),
        compiler_params=_params(("arbitrary", "arbitrary")),
        name="in_proj",
    )(xb, w)


def _retention_kernel(q_ref, k_ref, v_ref, g_ref, cos_ref, sin_ref, o_ref, state_ref, *, chunk):
    @pl.when(pl.program_id(1) == 0)
    def _():
        state_ref[...] = jnp.zeros_like(state_ref)

    c = chunk
    cos = cos_ref[0]
    sin = sin_ref[0]
    t_col = lax.broadcasted_iota(jnp.int32, (c, 1), 0).astype(F32)
    rel = (lax.broadcasted_iota(jnp.int32, (c, c), 0) - lax.broadcasted_iota(jnp.int32, (c, c), 1)).astype(F32)
    for h in range(N_HEADS):
        lg = math.log1p(-(2.0 ** (-RET_DECAY_OFFSET - h)))
        q = q_ref[0, :, h * HEAD_K:(h + 1) * HEAD_K].astype(F32)
        k = k_ref[0, :, h * HEAD_K:(h + 1) * HEAD_K].astype(F32)
        q = q * cos + pltpu.roll(q, HEAD_K // 2, axis=1) * sin
        k = (k * cos + pltpu.roll(k, HEAD_K // 2, axis=1) * sin) * (HEAD_K ** -0.5)
        v = v_ref[0, :, h * HEAD_V:(h + 1) * HEAD_V]
        intra = jnp.where(rel >= 0, jnp.exp(lg * rel), 0.0)
        scores = _dot_nt(q.astype(BF16), k.astype(BF16)) * intra
        st = state_ref[h]
        out = _dot(scores.astype(BF16), v) + _dot((q * jnp.exp(lg * (t_col + 1.0))).astype(BF16), st.astype(BF16))
        state_ref[h] = math.exp(lg * c) * st + _dot_tn((k * jnp.exp(lg * (c - 1.0 - t_col))).astype(BF16), v)
        mu = jnp.mean(out, axis=-1, keepdims=True)
        oc = out - mu
        var = jnp.mean(oc * oc, axis=-1, keepdims=True)
        gate = g_ref[0, :, h * HEAD_V:(h + 1) * HEAD_V].astype(F32)
        o_ref[0, :, h * HEAD_V:(h + 1) * HEAD_V] = (jax.nn.silu(gate) * oc * lax.rsqrt(var + NORM_EPS)).astype(o_ref.dtype)


def _retention(proj3, cos3, sin3):
    b, s, _ = proj3.shape
    c = min(RET_CHUNK, s)
    qk = lambda j: pl.BlockSpec((1, c, QK_WIDTH), lambda bi, n: (bi, n, j))
    vv = lambda j: pl.BlockSpec((1, c, V_WIDTH), lambda bi, n: (bi, n, j))
    tab = pl.BlockSpec((1, c, HEAD_K), lambda bi, n: (bi, n, 0))
    return pl.pallas_call(
        functools.partial(_retention_kernel, chunk=c),
        out_shape=jax.ShapeDtypeStruct((b, s, V_WIDTH), BF16),
        grid=(b, s // c),
        in_specs=[qk(COL_RQ // QK_WIDTH), qk(COL_RK // QK_WIDTH), vv(COL_RV // V_WIDTH), vv(COL_RG // V_WIDTH), tab, tab],
        out_specs=pl.BlockSpec((1, c, V_WIDTH), lambda bi, n: (bi, n, 0)),
        scratch_shapes=[pltpu.VMEM((N_HEADS, HEAD_K, HEAD_V), F32)],
        compiler_params=_params(("arbitrary", "arbitrary")),
        name="retention",
    )(proj3, proj3, proj3, proj3, cos3, sin3)


def _conv_silu(x, tail_ref, w):
    c = x.shape[0]
    prev = tail_ref[...]
    row8 = lax.broadcasted_iota(jnp.int32, (SUBLANES, 1), 0)
    y = w[3:4] * x
    for j in (1, 2, 3):
        xs = pltpu.roll(x, j, axis=0)
        head = jnp.where(row8 >= j, xs[:SUBLANES], pltpu.roll(prev, j, axis=0))
        xs = jnp.concatenate([head, xs[SUBLANES:]], axis=0)
        y = y + w[3 - j:4 - j] * xs
    tail_ref[...] = x[c - SUBLANES:]
    return y * jax.nn.sigmoid(y)


def _unit_lower_inverse(a):
    c = a.shape[0]
    eye = (lax.broadcasted_iota(jnp.int32, (c, c), 0) == lax.broadcasted_iota(jnp.int32, (c, c), 1)).astype(F32)
    p = -a
    inv = eye + p
    k = 1
    while 2 * k < c:
        pb = p.astype(BF16)
        p = _dot(pb, pb)
        inv = inv + _dot(inv.astype(BF16), p.astype(BF16))
        k *= 2
    return inv


def _gdn_kernel(qp_ref, kp_ref, vp_ref, z_ref, sm_ref, cwq_ref, cwk_ref, cwv_ref, alog_ref, dtb_ref, gn_ref,
                o_ref, state_ref, tq_ref, tk_ref, tv_ref, *, chunk):
    @pl.when(pl.program_id(1) == 0)
    def _():
        state_ref[...] = jnp.zeros_like(state_ref)
        tq_ref[...] = jnp.zeros_like(tq_ref)
        tk_ref[...] = jnp.zeros_like(tk_ref)
        tv_ref[...] = jnp.zeros_like(tv_ref)

    c = chunk
    q_all = _conv_silu(qp_ref[0].astype(F32), tq_ref, cwq_ref[...])
    k_all = _conv_silu(kp_ref[0].astype(F32), tk_ref, cwk_ref[...])
    v_all = _conv_silu(vp_ref[0].astype(F32), tv_ref, cwv_ref[...])
    sm = sm_ref[0]
    beta_all = jax.nn.sigmoid(sm)
    g_all = -jnp.exp(alog_ref[...]) * jax.nn.softplus(sm + dtb_ref[...])
    gc_all = _cumsum_rows(g_all)
    eg_all = jnp.exp(gc_all)
    gc_t = jnp.concatenate([gc_all, jnp.zeros((LANES - c, LANES), F32)], axis=0).T
    ri = lax.broadcasted_iota(jnp.int32, (c, c), 0)
    ci = lax.broadcasted_iota(jnp.int32, (c, c), 1)
    for h in range(N_HEADS):
        qh = q_all[:, h * HEAD_K:(h + 1) * HEAD_K]
        kh = k_all[:, h * HEAD_K:(h + 1) * HEAD_K]
        vh = v_all[:, h * HEAD_V:(h + 1) * HEAD_V]
        qn = qh * lax.rsqrt(jnp.sum(qh * qh, axis=-1, keepdims=True) + NORM_EPS) * (HEAD_K ** -0.5)
        kn = kh * lax.rsqrt(jnp.sum(kh * kh, axis=-1, keepdims=True) + NORM_EPS)
        beta = beta_all[:, h:h + 1]
        gc = gc_all[:, N_HEADS + h:N_HEADS + h + 1]
        gr = gc_t[N_HEADS + h:N_HEADS + h + 1, :c]
        eg = eg_all[:, N_HEADS + h:N_HEADS + h + 1]
        decay = jnp.where(ri >= ci, jnp.exp(gc - gr), 0.0)
        kb = kn * beta
        knb = kn.astype(BF16)
        a = jnp.where(ri > ci, _dot_nt(kb.astype(BF16), knb) * decay, 0.0)
        tm = _unit_lower_inverse(a).astype(BF16)
        u = _dot(tm, (vh * beta).astype(BF16))
        w = _dot(tm, (kb * eg).astype(BF16))
        st = state_ref[h]
        stb = st.astype(BF16)
        v_new = u - _dot(w.astype(BF16), stb)
        vnb = v_new.astype(BF16)
        attn = _dot_nt(qn.astype(BF16), knb) * decay
        out = _dot((qn * eg).astype(BF16), stb) + _dot(attn.astype(BF16), vnb)
        g_last = gc[c - 1:c]
        state_ref[h] = jnp.exp(g_last) * st + _dot_tn((kn * jnp.exp(g_last - gc)).astype(BF16), vnb)
        on = out * lax.rsqrt(jnp.mean(out * out, axis=-1, keepdims=True) + NORM_EPS) * gn_ref[...]
        z = z_ref[0, :, h * HEAD_V:(h + 1) * HEAD_V].astype(F32)
        o_ref[0, :, h * HEAD_V:(h + 1) * HEAD_V] = (jax.nn.silu(z) * on).astype(o_ref.dtype)


def _lane_row(vals, start):
    return jnp.zeros((1, LANES), F32).at[0, start:start + vals.shape[0]].set(vals.astype(F32))


def _gdn(proj3, small3, conv_w, a_log, dt_bias, norm_g):
    b, s, _ = proj3.shape
    c = min(GDN_CHUNK, s)
    qk = lambda j: pl.BlockSpec((1, c, QK_WIDTH), lambda bi, n: (bi, n, j))
    vv = lambda j: pl.BlockSpec((1, c, V_WIDTH), lambda bi, n: (bi, n, j))
    cw = lambda wd, j: pl.BlockSpec((4, wd), lambda bi, n: (0, j))
    row = lambda wd: pl.BlockSpec((1, wd), lambda bi, n: (0, 0))
    conv_w = conv_w.astype(F32)
    return pl.pallas_call(
        functools.partial(_gdn_kernel, chunk=c),
        out_shape=jax.ShapeDtypeStruct((b, s, V_WIDTH), BF16),
        grid=(b, s // c),
        in_specs=[qk(COL_GQ // QK_WIDTH), qk(COL_GK // QK_WIDTH), vv(COL_GV // V_WIDTH), vv(COL_GZ // V_WIDTH),
                  pl.BlockSpec((1, c, LANES), lambda bi, n: (bi, n, 0)),
                  cw(QK_WIDTH, 0), cw(QK_WIDTH, 1), cw(V_WIDTH, 1),
                  row(LANES), row(LANES), row(HEAD_V)],
        out_specs=pl.BlockSpec((1, c, V_WIDTH), lambda bi, n: (bi, n, 0)),
        scratch_shapes=[pltpu.VMEM((N_HEADS, HEAD_K, HEAD_V), F32),
                        pltpu.VMEM((SUBLANES, QK_WIDTH), F32), pltpu.VMEM((SUBLANES, QK_WIDTH), F32),
                        pltpu.VMEM((SUBLANES, V_WIDTH), F32)],
        compiler_params=_params(("arbitrary", "arbitrary")),
        name="gated_delta",
    )(proj3, proj3, proj3, proj3, small3, conv_w, conv_w, conv_w,
      _lane_row(a_log, N_HEADS), _lane_row(dt_bias, N_HEADS), norm_g.astype(F32)[None, :])


def _hgrn2_kernel(q_ref, f_ref, i_ref, g_ref, lbp_ref, gn_ref, o_ref, state_ref, *, chunk, layer):
    @pl.when(pl.program_id(1) == 0)
    def _():
        state_ref[...] = jnp.zeros_like(state_ref)

    c = chunk
    p = lbp_ref[...]
    e = jnp.exp(p - jnp.max(p, axis=0, keepdims=True))
    sm = e / jnp.sum(e, axis=0, keepdims=True)
    lb = jnp.zeros((1, QK_WIDTH), F32)
    for i in range(1, layer + 1):
        lb = lb + sm[i:i + 1]
    hf = f_ref[0].astype(F32)
    log_f = jnp.logaddexp(jax.nn.log_sigmoid(hf), jnp.log(lb) + jax.nn.log_sigmoid(-hf))
    k_all = (1.0 - lb) * jax.nn.sigmoid(-hf)
    q_all = q_ref[0].astype(F32) * (HEAD_K ** -0.5)
    gcum = _cumsum_rows(log_f)
    sub = HGRN_SUB
    lane = lax.broadcasted_iota(jnp.int32, (sub, sub), 1)
    srow = lax.broadcasted_iota(jnp.int32, (sub, 1), 0)
    for h in range(N_HEADS):
        g = gcum[:, h * HEAD_K:(h + 1) * HEAD_K]
        q = q_all[:, h * HEAD_K:(h + 1) * HEAD_K]
        k = k_all[:, h * HEAD_K:(h + 1) * HEAD_K]
        v = i_ref[0, :, h * HEAD_V:(h + 1) * HEAD_V]
        st = state_ref[h]
        inter = _dot_nt((q * jnp.exp(g)).astype(BF16), st.astype(BF16))
        blocks = []
        for i in range(c // sub):
            r0 = i * sub
            gb = g[r0:r0 + sub]
            qb = q[r0:r0 + sub]
            kb = k[r0:r0 + sub]
            acc_t = jnp.zeros((sub, sub), F32)
            for t in range(sub):
                w = kb * jnp.exp(jnp.minimum(gb[t:t + 1] - gb, 0.0)) * qb[t:t + 1]
                col = jnp.where(srow <= t, jnp.sum(w, axis=1, keepdims=True), 0.0)
                acc_t = jnp.where(lane == t, col, acc_t)
            ob = _dot_tn(acc_t.astype(BF16), v[r0:r0 + sub])
            if i > 0:
                ref = g[r0 - 1:r0]
                qt = qb * jnp.exp(gb - ref)
                kt = k[:r0] * jnp.exp(ref - g[:r0])
                sc = _dot_nt(qt.astype(BF16), kt.astype(BF16))
                ob = ob + _dot(sc.astype(BF16), v[:r0])
            blocks.append(ob)
        out = inter + jnp.concatenate(blocks, axis=0)
        g_last = g[c - 1:c]
        state_ref[h] = st * jnp.exp(g_last) + _dot_tn(v, (k * jnp.exp(g_last - g)).astype(BF16))
        on = out * lax.rsqrt(jnp.mean(out * out, axis=-1, keepdims=True) + NORM_EPS) * gn_ref[...]
        gate = g_ref[0, :, h * HEAD_V:(h + 1) * HEAD_V].astype(F32)
        o_ref[0, :, h * HEAD_V:(h + 1) * HEAD_V] = (jax.nn.sigmoid(gate) * on).astype(o_ref.dtype)


def _hgrn2(proj3, lb_param, norm_g, layer):
    b, s, _ = proj3.shape
    c = min(HGRN_CHUNK, s)
    qk = lambda j: pl.BlockSpec((1, c, QK_WIDTH), lambda bi, n: (bi, n, j))
    vv = lambda j: pl.BlockSpec((1, c, V_WIDTH), lambda bi, n: (bi, n, j))
    return pl.pallas_call(
        functools.partial(_hgrn2_kernel, chunk=c, layer=layer),
        out_shape=jax.ShapeDtypeStruct((b, s, V_WIDTH), BF16),
        grid=(b, s // c),
        in_specs=[qk(COL_HQ // QK_WIDTH), qk(COL_HF // QK_WIDTH), vv(COL_HI // V_WIDTH), vv(COL_HG // V_WIDTH),
                  pl.BlockSpec((DEPTH, QK_WIDTH), lambda bi, n: (0, 0)),
                  pl.BlockSpec((1, HEAD_V), lambda bi, n: (0, 0))],
        out_specs=pl.BlockSpec((1, c, V_WIDTH), lambda bi, n: (bi, n, 0)),
        scratch_shapes=[pltpu.VMEM((N_HEADS, HEAD_V, HEAD_K), F32)],
        compiler_params=_params(("arbitrary", "arbitrary")),
        name="hgrn2",
    )(proj3, proj3, proj3, proj3, lb_param.astype(F32), norm_g.astype(F32)[None, :])


def _merge_kernel(yr_ref, yg_ref, yh_ref, m0_ref, m1_ref, m2_ref, mb_ref, wb_ref, wo_ref, x_ref, lg_ref, lb_ref,
                  xo_ref, xb_ref):
    merged = None
    for n, (y_ref, m_ref) in enumerate(((yr_ref, m0_ref), (yg_ref, m1_ref), (yh_ref, m2_ref))):
        gate = jax.nn.sigmoid(m_ref[...].astype(F32) + mb_ref[:, n * D_MODEL:(n + 1) * D_MODEL])
        term = gate * _dot(y_ref[...], wb_ref[n])
        merged = term if merged is None else merged + term
    h = _dot(merged.astype(BF16), wo_ref[...])
    xn = _layer_norm(ALPHA * x_ref[...] + h, lg_ref[...], lb_ref[...])
    xo_ref[...] = xn
    xb_ref[...] = xn.astype(BF16)


def _merge(y_ret, y_gdn, y_hg, proj, merge_b, wb, wo, xf, ln_g, ln_b, tm=512):
    t = xf.shape[0]
    tm = min(tm, t)
    tile = lambda j: pl.BlockSpec((tm, D_MODEL), lambda i: (i, j))
    row = lambda wd: pl.BlockSpec((1, wd), lambda i: (0, 0))
    return pl.pallas_call(
        _merge_kernel,
        out_shape=(jax.ShapeDtypeStruct((t, D_MODEL), F32), jax.ShapeDtypeStruct((t, D_MODEL), BF16)),
        grid=(t // tm,),
        in_specs=[tile(0), tile(0), tile(0),
                  tile(COL_MG // D_MODEL), tile(COL_MG // D_MODEL + 1), tile(COL_MG // D_MODEL + 2),
                  row(N_BRANCH * D_MODEL),
                  pl.BlockSpec((N_BRANCH, V_WIDTH, D_MODEL), lambda i: (0, 0, 0)),
                  pl.BlockSpec((D_MODEL, D_MODEL), lambda i: (0, 0)),
                  tile(0), row(D_MODEL), row(D_MODEL)],
        out_specs=(tile(0), tile(0)),
        compiler_params=_params(("arbitrary",)),
        name="merge_out_ln",
    )(y_ret, y_gdn, y_hg, proj, proj, proj, merge_b.astype(F32)[None, :], wb, wo, xf,
      ln_g.astype(F32)[None, :], ln_b.astype(F32)[None, :])


def _router_kernel(x_ref, rwt_ref, rb_ref, ct_ref):
    logits = lax.dot_general(rwt_ref[...], x_ref[...], (((1,), (1,)), ((), ())),
                             precision=lax.Precision.HIGHEST, preferred_element_type=F32)
    scores = jax.nn.sigmoid(logits)
    biased = scores + rb_ref[...]
    epg = EXPERTS_PER_GROUP
    brow = [biased[e:e + 1] for e in range(N_EXPERTS)]
    srow = [scores[e:e + 1] for e in range(N_EXPERTS)]
    gscore = []
    for g in range(N_GROUPS):
        a, b, c, d = brow[g * epg:(g + 1) * epg]
        hi1, lo1, hi2, lo2 = jnp.maximum(a, b), jnp.minimum(a, b), jnp.maximum(c, d), jnp.minimum(c, d)
        gscore.append(jnp.maximum(hi1, hi2) + jnp.maximum(jnp.minimum(hi1, hi2), jnp.maximum(lo1, lo2)))
    best = jnp.zeros(gscore[0].shape, jnp.int32)
    top = gscore[0]
    for g in range(1, N_GROUPS):
        upd = gscore[g] > top
        best = jnp.where(upd, g, best)
        top = jnp.where(upd, gscore[g], top)

    def pick(rows, j):
        out = rows[(N_GROUPS - 1) * epg + j]
        for g in range(N_GROUPS - 2, -1, -1):
            out = jnp.where(best == g, rows[g * epg + j], out)
        return out

    bv = [pick(brow, j) for j in range(epg)]
    sv = [pick(srow, j) for j in range(epg)]
    chosen = []
    for j in range(epg):
        rank = jnp.zeros(best.shape, jnp.int32)
        for i in range(epg):
            if i == j:
                continue
            ahead = (bv[i] > bv[j]) if i > j else (bv[i] >= bv[j])
            rank = rank + ahead.astype(jnp.int32)
        chosen.append(rank < 2)
    den = None
    for j in range(epg):
        term = jnp.where(chosen[j], sv[j], 0.0)
        den = term if den is None else den + term
    for g in range(N_GROUPS):
        for j in range(epg):
            wgt = jnp.where(jnp.logical_and(chosen[j], best == g), sv[j] / den, 0.0)
            ct_ref[g * epg + j:g * epg + j + 1, :] = wgt


def _router(xf, router_w, router_b, tm=2048):
    t = xf.shape[0]
    tm = min(tm, t)
    return pl.pallas_call(
        _router_kernel,
        out_shape=jax.ShapeDtypeStruct((N_EXPERTS, t), F32),
        grid=(t // tm,),
        in_specs=[pl.BlockSpec((tm, D_MODEL), lambda i: (i, 0)),
                  pl.BlockSpec((N_EXPERTS, D_MODEL), lambda i: (0, 0)),
                  pl.BlockSpec((N_EXPERTS, 1), lambda i: (0, 0))],
        out_specs=pl.BlockSpec((N_EXPERTS, tm), lambda i: (0, i)),
        compiler_params=_params(("arbitrary",)),
        name="router",
    )(xf, router_w.astype(F32).T, router_b.astype(F32)[:, None])


def _moe_kernel(xb_ref, wg_ref, wu_ref, wd_ref, c_ref, x_ref, lg_ref, lb_ref, xo_ref, xbo_ref, acc_ref):
    e = pl.program_id(1)

    @pl.when(e == 0)
    def _():
        acc_ref[...] = jnp.zeros_like(acc_ref)

    xb = xb_ref[...]
    hmid = jax.nn.silu(_dot(xb, wg_ref[0])) * _dot(xb, wu_ref[0])
    comb = c_ref[...]
    lane = lax.broadcasted_iota(jnp.int32, comb.shape, 1)
    wgt = jnp.sum(jnp.where(lane == e, comb, 0.0), axis=1, keepdims=True)
    acc_ref[...] += wgt * _dot(hmid.astype(BF16), wd_ref[0])

    @pl.when(e == pl.num_programs(1) - 1)
    def _():
        xn = _layer_norm(ALPHA * x_ref[...] + acc_ref[...], lg_ref[...], lb_ref[...])
        xo_ref[...] = xn
        xbo_ref[...] = xn.astype(BF16)


def _moe(xb, wg, wu, wd, comb, xf, ln_g, ln_b, tm=1024):
    t = xf.shape[0]
    tm = min(tm, t)
    tile = pl.BlockSpec((tm, D_MODEL), lambda i, e: (i, 0))
    row = pl.BlockSpec((1, D_MODEL), lambda i, e: (0, 0))
    return pl.pallas_call(
        _moe_kernel,
        out_shape=(jax.ShapeDtypeStruct((t, D_MODEL), F32), jax.ShapeDtypeStruct((t, D_MODEL), BF16)),
        grid=(t // tm, N_EXPERTS),
        in_specs=[tile,
                  pl.BlockSpec((1, D_MODEL, D_EXPERT), lambda i, e: (e, 0, 0)),
                  pl.BlockSpec((1, D_MODEL, D_EXPERT), lambda i, e: (e, 0, 0)),
                  pl.BlockSpec((1, D_EXPERT, D_MODEL), lambda i, e: (e, 0, 0)),
                  pl.BlockSpec((tm, N_EXPERTS), lambda i, e: (i, 0)),
                  tile, row, row],
        out_specs=(tile, tile),
        scratch_shapes=[pltpu.VMEM((tm, D_MODEL), F32)],
        compiler_params=_params(("arbitrary", "arbitrary")),
        name="moe_ln",
    )(xb, wg, wu, wd, comb, xf, ln_g.astype(F32)[None, :], ln_b.astype(F32)[None, :])


def kernel(x, positions, w_in, gdn_conv_w, gdn_a_log, gdn_dt_bias, gdn_norm_g, hgrn_lb, hgrn_norm_g, merge_b,
           w_branch, w_out, ln1_g, ln1_b, router_w, router_b, moe_w_gate, moe_w_up, moe_w_down, ln2_g, ln2_b):
    b, s, d = x.shape
    t = b * s
    assert d == D_MODEL and w_in.shape[-1] == N_MAIN + SMALL_WIDTH

    w_main = jnp.concatenate([w_in[:, :, :SMALL_START], w_in[:, :, SMALL_START + SMALL_WIDTH:]], axis=-1).astype(BF16)
    w_small = jnp.pad(w_in[:, :, SMALL_START:SMALL_START + SMALL_WIDTH], ((0, 0), (0, 0), (0, LANES - SMALL_WIDTH))).astype(BF16)
    wb = w_branch.astype(BF16)
    wo = w_out.astype(BF16)
    wg = moe_w_gate.astype(BF16)
    wu = moe_w_up.astype(BF16)
    wd = moe_w_down.astype(BF16)

    cos, sin = _rope_tables(positions)
    cos3 = cos.reshape(b, s, HEAD_K)
    sin3 = sin.reshape(b, s, HEAD_K)

    xf = x.reshape(t, d).astype(F32)
    xb = xf.astype(BF16)
    for l in range(DEPTH):
        proj = _in_proj(xb, w_main[l], BF16, tm=2048, tn=768)
        small = _in_proj(xb, w_small[l], F32, tm=2048, tn=LANES)
        proj3 = proj.reshape(b, s, N_MAIN)
        y_ret = _retention(proj3, cos3, sin3)
        y_gdn = _gdn(proj3, small.reshape(b, s, LANES), gdn_conv_w[l], gdn_a_log[l], gdn_dt_bias[l], gdn_norm_g[l])
        y_hg = _hgrn2(proj3, hgrn_lb, hgrn_norm_g[l], l)
        xf, xb = _merge(y_ret.reshape(t, V_WIDTH), y_gdn.reshape(t, V_WIDTH), y_hg.reshape(t, V_WIDTH), proj,
                        merge_b[l], wb[l], wo[l], xf, ln1_g[l], ln1_b[l])
        comb = _router(xf, router_w, router_b).T
        xf, xb = _moe(xb, wg[l], wu[l], wd[l], comb, xf, ln2_g[l], ln2_b[l])
    return xf.reshape(b, s, d).astype(x.dtype)
```

```python
import functools
import math

import jax
import jax.numpy as jnp
from jax import lax
from jax.experimental import pallas as pl
from jax.experimental.pallas import tpu as pltpu

F32 = jnp.float32
BF16 = jnp.bfloat16

D_MODEL = 1024
DEPTH = 4
N_HEADS = 4
HEAD_V = 256
HEAD_K = 128
QK_WIDTH = N_HEADS * HEAD_K
V_WIDTH = N_HEADS * HEAD_V
N_BRANCH = 3
ROPE_BASE = 10000.0
RET_DECAY_OFFSET = 5.0
N_EXPERTS = 16
N_GROUPS = 4
EXPERTS_PER_GROUP = N_EXPERTS // N_GROUPS
D_EXPERT = D_MODEL // 2
ALPHA = (2 * DEPTH) ** 0.25
LN_EPS = 1e-5
NORM_EPS = 1e-6

SMALL_START = 2 * QK_WIDTH + 2 * V_WIDTH + (2 * QK_WIDTH + V_WIDTH) + V_WIDTH
SMALL_WIDTH = 2 * N_HEADS
N_MAIN = 12288
COL_RQ, COL_RK, COL_RV, COL_RG = 0, 512, 1024, 2048
COL_GQ, COL_GK, COL_GV, COL_GZ = 3072, 3584, 4096, 5120
COL_HQ, COL_HF, COL_HI, COL_HG = 6144, 6656, 7168, 8192
COL_MG = 9216

LANES = 128
SUBLANES = 8
VMEM_LIMIT = 56 * 1024 * 1024

RET_CHUNK = 256
CHUNK = 64
STEP_TOKENS = 256
HGRN_SUB = 16


def _dot(a, b):
    return jnp.dot(a, b, preferred_element_type=F32)


def _dot_nt(a, b):
    return lax.dot_general(a, b, (((1,), (1,)), ((), ())), preferred_element_type=F32)


def _dot_tn(a, b):
    return lax.dot_general(a, b, (((0,), (0,)), ((), ())), preferred_element_type=F32)


def _cumsum_rows(x, seg):
    assert seg & (seg - 1) == 0
    pos = lax.broadcasted_iota(jnp.int32, x.shape, 0) & (seg - 1)
    s = 1
    while s < seg:
        x = x + jnp.where(pos >= s, pltpu.roll(x, s, axis=0), 0.0)
        s *= 2
    return x


def _layer_norm(z, g, b):
    mu = jnp.mean(z, axis=-1, keepdims=True)
    zc = z - mu
    var = jnp.mean(zc * zc, axis=-1, keepdims=True)
    return zc * lax.rsqrt(var + LN_EPS) * g + b


def _params(sem):
    return pltpu.CompilerParams(dimension_semantics=sem, vmem_limit_bytes=VMEM_LIMIT)


def _rope_kernel(pos_ref, invf_ref, cos_ref, sin_ref):
    ang = pos_ref[...].astype(F32) * invf_ref[...]
    lane = lax.broadcasted_iota(jnp.int32, ang.shape, 1)
    cos_ref[...] = jnp.cos(ang)
    s = jnp.sin(ang)
    sin_ref[...] = jnp.where(lane < HEAD_K // 2, -s, s)


def _rope_tables(positions):
    t = positions.size
    half = HEAD_K // 2
    inv_freq = 1.0 / (ROPE_BASE ** jnp.linspace(0.0, 1.0, half, dtype=F32))
    invf = jnp.concatenate([inv_freq, inv_freq])[None, :]
    ts = min(t, 2048)
    return pl.pallas_call(
        _rope_kernel,
        out_shape=(jax.ShapeDtypeStruct((t, HEAD_K), F32), jax.ShapeDtypeStruct((t, HEAD_K), F32)),
        grid=(t // ts,),
        in_specs=[pl.BlockSpec((ts, 1), lambda i: (i, 0)), pl.BlockSpec((1, HEAD_K), lambda i: (0, 0))],
        out_specs=(pl.BlockSpec((ts, HEAD_K), lambda i: (i, 0)), pl.BlockSpec((ts, HEAD_K), lambda i: (i, 0))),
        compiler_params=_params(("arbitrary",)),
        name="rope_tables",
    )(positions.reshape(t, 1), invf)


def _matmul_kernel(x_ref, w_ref, o_ref):
    o_ref[...] = _dot(x_ref[...], w_ref[...]).astype(o_ref.dtype)


def _in_proj(xb, w, out_dtype, tm, tn):
    t, d = xb.shape
    n = w.shape[1]
    tm = min(tm, t)
    tn = min(tn, n)
    return pl.pallas_call(
        _matmul_kernel,
        out_shape=jax.ShapeDtypeStruct((t, n), out_dtype),
        grid=(t // tm, n // tn),
        in_specs=[pl.BlockSpec((tm, d), lambda i, j: (i, 0)), pl.BlockSpec((d, tn), lambda i, j: (0, j))],
        out_specs=pl.BlockSpec((tm, tn), lambda i, j: (i, j)),
        compiler_params=_params(("arbitrary", "arbitrary")),
        name="in_proj",
    )(xb, w)


def _retention_kernel(q_ref, k_ref, v_ref, g_ref, cos_ref, sin_ref, o_ref, state_ref, *, chunk):
    @pl.when(pl.program_id(1) == 0)
    def _():
        state_ref[...] = jnp.zeros_like(state_ref)

    c = chunk
    cos = cos_ref[0]
    sin = sin_ref[0]
    t_col = lax.broadcasted_iota(jnp.int32, (c, 1), 0).astype(F32)
    rel = (lax.broadcasted_iota(jnp.int32, (c, c), 0) - lax.broadcasted_iota(jnp.int32, (c, c), 1)).astype(F32)
    for h in range(N_HEADS):
        lg = math.log1p(-(2.0 ** (-RET_DECAY_OFFSET - h)))
        q = q_ref[0, :, h * HEAD_K:(h + 1) * HEAD_K].astype(F32)
        k = k_ref[0, :, h * HEAD_K:(h + 1) * HEAD_K].astype(F32)
        q = q * cos + pltpu.roll(q, HEAD_K // 2, axis=1) * sin
        k = (k * cos + pltpu.roll(k, HEAD_K // 2, axis=1) * sin) * (HEAD_K ** -0.5)
        v = v_ref[0, :, h * HEAD_V:(h + 1) * HEAD_V]
        intra = jnp.where(rel >= 0, jnp.exp(lg * rel), 0.0)
        scores = _dot_nt(q.astype(BF16), k.astype(BF16)) * intra
        st = state_ref[h]
        out = _dot(scores.astype(BF16), v) + _dot((q * jnp.exp(lg * (t_col + 1.0))).astype(BF16), st.astype(BF16))
        state_ref[h] = math.exp(lg * c) * st + _dot_tn((k * jnp.exp(lg * (c - 1.0 - t_col))).astype(BF16), v)
        mu = jnp.mean(out, axis=-1, keepdims=True)
        oc = out - mu
        var = jnp.mean(oc * oc, axis=-1, keepdims=True)
        gate = g_ref[0, :, h * HEAD_V:(h + 1) * HEAD_V].astype(F32)
        o_ref[0, :, h * HEAD_V:(h + 1) * HEAD_V] = (jax.nn.silu(gate) * oc * lax.rsqrt(var + NORM_EPS)).astype(o_ref.dtype)


def _retention(proj3, cos3, sin3):
    b, s, _ = proj3.shape
    c = min(RET_CHUNK, s)
    qk = lambda j: pl.BlockSpec((1, c, QK_WIDTH), lambda bi, n: (bi, n, j))
    vv = lambda j: pl.BlockSpec((1, c, V_WIDTH), lambda bi, n: (bi, n, j))
    tab = pl.BlockSpec((1, c, HEAD_K), lambda bi, n: (bi, n, 0))
    return pl.pallas_call(
        functools.partial(_retention_kernel, chunk=c),
        out_shape=jax.ShapeDtypeStruct((b, s, V_WIDTH), BF16),
        grid=(b, s // c),
        in_specs=[qk(COL_RQ // QK_WIDTH), qk(COL_RK // QK_WIDTH), vv(COL_RV // V_WIDTH), vv(COL_RG // V_WIDTH), tab, tab],
        out_specs=pl.BlockSpec((1, c, V_WIDTH), lambda bi, n: (bi, n, 0)),
        scratch_shapes=[pltpu.VMEM((N_HEADS, HEAD_K, HEAD_V), F32)],
        compiler_params=_params(("arbitrary", "arbitrary")),
        name="retention",
    )(proj3, proj3, proj3, proj3, cos3, sin3)


def _conv_silu(x, tail_ref, w):
    c = x.shape[0]
    prev = tail_ref[...]
    row8 = lax.broadcasted_iota(jnp.int32, (SUBLANES, 1), 0)
    y = w[3:4] * x
    for j in (1, 2, 3):
        xs = pltpu.roll(x, j, axis=0)
        head = jnp.where(row8 >= j, xs[:SUBLANES], pltpu.roll(prev, j, axis=0))
        xs = jnp.concatenate([head, xs[SUBLANES:]], axis=0)
        y = y + w[3 - j:4 - j] * xs
    tail_ref[...] = x[c - SUBLANES:]
    return y * jax.nn.sigmoid(y)


def _unit_lower_inverses(mats, order):
    n = mats[0].shape[0]
    eye = (lax.broadcasted_iota(jnp.int32, (n, n), 0) == lax.broadcasted_iota(jnp.int32, (n, n), 1)).astype(F32)
    ps = [-a for a in mats]
    invs = [eye + p for p in ps]
    k = 1
    while 2 * k < order:
        pbs = [p.astype(BF16) for p in ps]
        ps = [_dot(pb, pb) for pb in pbs]
        invs = [inv + _dot(inv.astype(BF16), p.astype(BF16)) for inv, p in zip(invs, ps)]
        k *= 2
    return invs


def _stack_heads(x, rows, width):
    return jnp.concatenate([x[rows, h * width:(h + 1) * width] for h in range(N_HEADS)], axis=0)


def _gdn_kernel(qp_ref, kp_ref, vp_ref, z_ref, sm_ref, cwq_ref, cwk_ref, cwv_ref, alog_ref, dtb_ref, gn_ref,
                o_ref, state_ref, tq_ref, tk_ref, tv_ref, qs_ref, os_ref, ms_ref, ns_ref, *, n_chunks):
    @pl.when(pl.program_id(1) == 0)
    def _():
        state_ref[...] = jnp.zeros_like(state_ref)
        tq_ref[...] = jnp.zeros_like(tq_ref)
        tk_ref[...] = jnp.zeros_like(tk_ref)
        tv_ref[...] = jnp.zeros_like(tv_ref)

    c = CHUNK
    hc = N_HEADS * c
    q_all = _conv_silu(qp_ref[0].astype(F32), tq_ref, cwq_ref[...])
    k_all = _conv_silu(kp_ref[0].astype(F32), tk_ref, cwk_ref[...])
    v_all = _conv_silu(vp_ref[0].astype(F32), tv_ref, cwv_ref[...])
    sm = sm_ref[0]
    beta_all = jax.nn.sigmoid(sm)
    g_all = -jnp.exp(alog_ref[...]) * jax.nn.softplus(sm + dtb_ref[...])
    gc_all = _cumsum_rows(g_all, c)
    gc_t = gc_all.T
    ri = lax.broadcasted_iota(jnp.int32, (hc, hc), 0)
    ci = lax.broadcasted_iota(jnp.int32, (hc, hc), 1)
    shift = c.bit_length() - 1
    lower = jnp.logical_and(ri >> shift == ci >> shift, ri >= ci)
    diag = ri == ci
    row_head = lax.broadcasted_iota(jnp.int32, (hc, 1), 0) >> shift
    col = lambda x, rows, j: jnp.concatenate([x[rows, j + h:j + h + 1] for h in range(N_HEADS)], axis=0)

    pre = []
    for j in range(n_chunks):
        rows = slice(j * c, (j + 1) * c)
        kraw = _stack_heads(k_all, rows, HEAD_K)
        qraw = _stack_heads(q_all, rows, HEAD_K)
        kn = kraw * lax.rsqrt(jnp.sum(kraw * kraw, axis=-1, keepdims=True) + NORM_EPS)
        qn = qraw * lax.rsqrt(jnp.sum(qraw * qraw, axis=-1, keepdims=True) + NORM_EPS) * (HEAD_K ** -0.5)
        beta = col(beta_all, rows, 0)
        gc = col(gc_all, rows, N_HEADS)
        gr = jnp.concatenate([gc_t[N_HEADS + h:N_HEADS + h + 1, rows] for h in range(N_HEADS)], axis=1)
        last = slice((j + 1) * c - 1, (j + 1) * c)
        g_last = [gc_all[last, N_HEADS + h:N_HEADS + h + 1] for h in range(N_HEADS)]
        gl = jnp.concatenate([jnp.broadcast_to(g, (c, 1)) for g in g_last], axis=0)
        pre.append(dict(kn=kn, qn=qn, kb=kn * beta, vb=_stack_heads(v_all, rows, HEAD_V) * beta, gc=gc, gr=gr,
                        eg=jnp.exp(gc), gl=gl, egl=[jnp.exp(g) for g in g_last]))
    decays = [jnp.where(lower, jnp.exp(p["gc"] - p["gr"]), 0.0) for p in pre]
    kqs = [_dot_nt(jnp.concatenate([p["kb"], p["qn"]], axis=0).astype(BF16), p["kn"].astype(BF16)) for p in pre]
    tms = _unit_lower_inverses([jnp.where(diag, 0.0, kq[:hc] * d) for kq, d in zip(kqs, decays)], c)
    uws = [_dot(tm.astype(BF16), jnp.concatenate([p["vb"], p["kb"] * p["eg"]], axis=1).astype(BF16)).astype(BF16)
           for tm, p in zip(tms, pre)]
    aos = [_dot((kq[hc:] * d).astype(BF16), uw) for kq, d, uw in zip(kqs, decays, uws)]
    for j, (p, uw, ao) in enumerate(zip(pre, uws, aos)):
        kd = p["kn"] * jnp.exp(p["gl"] - p["gc"])
        kd_wide = jnp.concatenate([jnp.where(row_head == h, kd, 0.0) for h in range(N_HEADS)], axis=1)
        nm = _dot_tn(kd_wide.astype(BF16), uw)
        qs_ref[j] = (p["qn"] * p["eg"] - ao[:, HEAD_V:]).astype(BF16)
        os_ref[j] = ao[:, :HEAD_V]
        ns_ref[j] = nm[:, :HEAD_V]
        ms_ref[j] = nm[:, HEAD_V:].astype(BF16)

    for j in range(n_chunks):
        sts = [state_ref[h] for h in range(N_HEADS)]
        rrs = [_dot(jnp.concatenate([qs_ref[j, h * c:(h + 1) * c], ms_ref[j, h * HEAD_K:(h + 1) * HEAD_K]], axis=0),
                    sts[h].astype(BF16)) for h in range(N_HEADS)]
        for h in range(N_HEADS):
            state_ref[h] = pre[j]["egl"][h] * sts[h] - rrs[h][c:] + ns_ref[j, h * HEAD_K:(h + 1) * HEAD_K]
        for h in range(N_HEADS):
            out = rrs[h][:c] + os_ref[j, h * c:(h + 1) * c]
            on = out * lax.rsqrt(jnp.mean(out * out, axis=-1, keepdims=True) + NORM_EPS) * gn_ref[...]
            z = z_ref[0, j * c:(j + 1) * c, h * HEAD_V:(h + 1) * HEAD_V].astype(F32)
            o_ref[0, j * c:(j + 1) * c, h * HEAD_V:(h + 1) * HEAD_V] = (jax.nn.silu(z) * on).astype(o_ref.dtype)


def _lane_row(vals, start):
    return jnp.zeros((1, LANES), F32).at[0, start:start + vals.shape[0]].set(vals.astype(F32))


def _gdn(proj3, small3, conv_w, a_log, dt_bias, norm_g):
    b, s, _ = proj3.shape
    c = min(STEP_TOKENS, s)
    g = c // CHUNK
    hc = N_HEADS * CHUNK
    qk = lambda j: pl.BlockSpec((1, c, QK_WIDTH), lambda bi, n: (bi, n, j))
    vv = lambda j: pl.BlockSpec((1, c, V_WIDTH), lambda bi, n: (bi, n, j))
    cw = lambda wd, j: pl.BlockSpec((4, wd), lambda bi, n: (0, j))
    row = lambda wd: pl.BlockSpec((1, wd), lambda bi, n: (0, 0))
    conv_w = conv_w.astype(F32)
    return pl.pallas_call(
        functools.partial(_gdn_kernel, n_chunks=g),
        out_shape=jax.ShapeDtypeStruct((b, s, V_WIDTH), BF16),
        grid=(b, s // c),
        in_specs=[qk(COL_GQ // QK_WIDTH), qk(COL_GK // QK_WIDTH), vv(COL_GV // V_WIDTH), vv(COL_GZ // V_WIDTH),
                  pl.BlockSpec((1, c, LANES), lambda bi, n: (bi, n, 0)),
                  cw(QK_WIDTH, 0), cw(QK_WIDTH, 1), cw(V_WIDTH, 1),
                  row(LANES), row(LANES), row(HEAD_V)],
        out_specs=pl.BlockSpec((1, c, V_WIDTH), lambda bi, n: (bi, n, 0)),
        scratch_shapes=[pltpu.VMEM((N_HEADS, HEAD_K, HEAD_V), F32),
                        pltpu.VMEM((SUBLANES, QK_WIDTH), F32), pltpu.VMEM((SUBLANES, QK_WIDTH), F32),
                        pltpu.VMEM((SUBLANES, V_WIDTH), F32),
                        pltpu.VMEM((g, hc, HEAD_K), BF16), pltpu.VMEM((g, hc, HEAD_V), F32),
                        pltpu.VMEM((g, N_HEADS * HEAD_K, HEAD_K), BF16), pltpu.VMEM((g, N_HEADS * HEAD_K, HEAD_V), F32)],
        compiler_params=_params(("arbitrary", "arbitrary")),
        name="gated_delta",
    )(proj3, proj3, proj3, proj3, small3, conv_w, conv_w, conv_w,
      _lane_row(a_log, N_HEADS), _lane_row(dt_bias, N_HEADS), norm_g.astype(F32)[None, :])


def _hgrn2_kernel(q_ref, f_ref, i_ref, g_ref, lbp_ref, gn_ref, o_ref, state_ref, *, n_chunks, layer):
    @pl.when(pl.program_id(1) == 0)
    def _():
        state_ref[...] = jnp.zeros_like(state_ref)

    c = CHUNK
    p = lbp_ref[...]
    e = jnp.exp(p - jnp.max(p, axis=0, keepdims=True))
    sm = e / jnp.sum(e, axis=0, keepdims=True)
    lb = jnp.zeros((1, QK_WIDTH), F32)
    for i in range(1, layer + 1):
        lb = lb + sm[i:i + 1]
    hf = f_ref[0].astype(F32)
    log_f = jnp.logaddexp(jax.nn.log_sigmoid(hf), jnp.log(lb) + jax.nn.log_sigmoid(-hf))
    k_all = (1.0 - lb) * jax.nn.sigmoid(-hf)
    q_all = q_ref[0].astype(F32) * (HEAD_K ** -0.5)
    gcum = _cumsum_rows(log_f, c)
    sub = HGRN_SUB
    nsub = c // sub
    sub_shift = sub.bit_length() - 1
    lane = lax.broadcasted_iota(jnp.int32, (sub, c), 1)
    srow = lax.broadcasted_iota(jnp.int32, (sub, 1), 0)
    crow = lax.broadcasted_iota(jnp.int32, (c, 1), 0)
    cblk = crow >> sub_shift
    for j in range(n_chunks):
        rows = slice(j * c, (j + 1) * c)
        for h in range(N_HEADS):
            g = gcum[rows, h * HEAD_K:(h + 1) * HEAD_K]
            q = q_all[rows, h * HEAD_K:(h + 1) * HEAD_K]
            k = k_all[rows, h * HEAD_K:(h + 1) * HEAD_K]
            v = i_ref[0, rows, h * HEAD_V:(h + 1) * HEAD_V]
            refs = [g[i * sub - 1:i * sub] for i in range(1, nsub)]
            ref_rows = jnp.concatenate([jnp.zeros((sub, HEAD_K), F32)] + [jnp.broadcast_to(r, (sub, HEAD_K)) for r in refs], axis=0)
            qt = q * jnp.exp(g - ref_rows)
            q_wide = jnp.concatenate([jnp.where(cblk == i, qt, 0.0) for i in range(1, nsub)], axis=1)
            k_wide = jnp.concatenate([jnp.where(crow < i * sub, k * jnp.exp(jnp.minimum(r - g, 0.0)), 0.0)
                                      for i, r in zip(range(1, nsub), refs)], axis=1)
            off_t = _dot_nt(k_wide.astype(BF16), q_wide.astype(BF16))
            blocks = []
            for i in range(nsub):
                r0 = i * sub
                gb = g[r0:r0 + sub]
                qb = q[r0:r0 + sub]
                kb = k[r0:r0 + sub]
                acc_t = jnp.zeros((sub, c), F32)
                for t in range(sub):
                    w = kb * jnp.exp(gb[t:t + 1] - gb) * qb[t:t + 1]
                    col = jnp.where(srow <= t, jnp.sum(w, axis=1, keepdims=True), 0.0)
                    acc_t = jnp.where((lane & (sub - 1)) == t, col, acc_t)
                blocks.append(jnp.where((lane >> sub_shift) == i, acc_t, off_t[r0:r0 + sub]))
            attn_t = jnp.concatenate(blocks, axis=0)
            st = state_ref[h]
            out = _dot_tn(attn_t.astype(BF16), v) + _dot_nt((q * jnp.exp(g)).astype(BF16), st.astype(BF16))
            g_last = g[c - 1:c]
            state_ref[h] = st * jnp.exp(g_last) + _dot_tn(v, (k * jnp.exp(g_last - g)).astype(BF16))
            on = out * lax.rsqrt(jnp.mean(out * out, axis=-1, keepdims=True) + NORM_EPS) * gn_ref[...]
            gate = g_ref[0, rows, h * HEAD_V:(h + 1) * HEAD_V].astype(F32)
            o_ref[0, rows, h * HEAD_V:(h + 1) * HEAD_V] = (jax.nn.sigmoid(gate) * on).astype(o_ref.dtype)


def _hgrn2(proj3, lb_param, norm_g, layer):
    b, s, _ = proj3.shape
    c = min(STEP_TOKENS, s)
    qk = lambda j: pl.BlockSpec((1, c, QK_WIDTH), lambda bi, n: (bi, n, j))
    vv = lambda j: pl.BlockSpec((1, c, V_WIDTH), lambda bi, n: (bi, n, j))
    return pl.pallas_call(
        functools.partial(_hgrn2_kernel, n_chunks=c // CHUNK, layer=layer),
        out_shape=jax.ShapeDtypeStruct((b, s, V_WIDTH), BF16),
        grid=(b, s // c),
        in_specs=[qk(COL_HQ // QK_WIDTH), qk(COL_HF // QK_WIDTH), vv(COL_HI // V_WIDTH), vv(COL_HG // V_WIDTH),
                  pl.BlockSpec((DEPTH, QK_WIDTH), lambda bi, n: (0, 0)),
                  pl.BlockSpec((1, HEAD_V), lambda bi, n: (0, 0))],
        out_specs=pl.BlockSpec((1, c, V_WIDTH), lambda bi, n: (bi, n, 0)),
        scratch_shapes=[pltpu.VMEM((N_HEADS, HEAD_V, HEAD_K), F32)],
        compiler_params=_params(("arbitrary", "arbitrary")),
        name="hgrn2",
    )(proj3, proj3, proj3, proj3, lb_param.astype(F32), norm_g.astype(F32)[None, :])


def _merge_kernel(yr_ref, yg_ref, yh_ref, m0_ref, m1_ref, m2_ref, mb_ref, wb_ref, wo_ref, x_ref, lg_ref, lb_ref,
                  xo_ref, xb_ref):
    merged = None
    for n, (y_ref, m_ref) in enumerate(((yr_ref, m0_ref), (yg_ref, m1_ref), (yh_ref, m2_ref))):
        gate = jax.nn.sigmoid(m_ref[...].astype(F32) + mb_ref[:, n * D_MODEL:(n + 1) * D_MODEL])
        term = gate * _dot(y_ref[...], wb_ref[n])
        merged = term if merged is None else merged + term
    h = _dot(merged.astype(BF16), wo_ref[...])
    xn = _layer_norm(ALPHA * x_ref[...] + h, lg_ref[...], lb_ref[...])
    xo_ref[...] = xn
    xb_ref[...] = xn.astype(BF16)


def _merge(y_ret, y_gdn, y_hg, proj, merge_b, wb, wo, xf, ln_g, ln_b, tm=512):
    t = xf.shape[0]
    tm = min(tm, t)
    tile = lambda j: pl.BlockSpec((tm, D_MODEL), lambda i: (i, j))
    row = lambda wd: pl.BlockSpec((1, wd), lambda i: (0, 0))
    return pl.pallas_call(
        _merge_kernel,
        out_shape=(jax.ShapeDtypeStruct((t, D_MODEL), F32), jax.ShapeDtypeStruct((t, D_MODEL), BF16)),
        grid=(t // tm,),
        in_specs=[tile(0), tile(0), tile(0),
                  tile(COL_MG // D_MODEL), tile(COL_MG // D_MODEL + 1), tile(COL_MG // D_MODEL + 2),
                  row(N_BRANCH * D_MODEL),
                  pl.BlockSpec((N_BRANCH, V_WIDTH, D_MODEL), lambda i: (0, 0, 0)),
                  pl.BlockSpec((D_MODEL, D_MODEL), lambda i: (0, 0)),
                  tile(0), row(D_MODEL), row(D_MODEL)],
        out_specs=(tile(0), tile(0)),
        compiler_params=_params(("arbitrary",)),
        name="merge_out_ln",
    )(y_ret, y_gdn, y_hg, proj, proj, proj, merge_b.astype(F32)[None, :], wb, wo, xf,
      ln_g.astype(F32)[None, :], ln_b.astype(F32)[None, :])


def _router_kernel(x_ref, rw_ref, rb_ref, ct_ref):
    x = x_ref[...]
    w = rw_ref[...]
    xh = x.astype(BF16)
    xl = (x - xh.astype(F32)).astype(BF16)
    wh = w.astype(BF16)
    wl = (w - wh.astype(F32)).astype(BF16)
    logits = (_dot(xh, wh) + (_dot(xl, wh) + _dot(xh, wl))).T[:N_EXPERTS]
    scores = jax.nn.sigmoid(logits)
    biased = scores + rb_ref[...]
    epg = EXPERTS_PER_GROUP
    brow = [biased[e:e + 1] for e in range(N_EXPERTS)]
    srow = [scores[e:e + 1] for e in range(N_EXPERTS)]
    gscore = []
    for g in range(N_GROUPS):
        a, b, c, d = brow[g * epg:(g + 1) * epg]
        hi1, lo1, hi2, lo2 = jnp.maximum(a, b), jnp.minimum(a, b), jnp.maximum(c, d), jnp.minimum(c, d)
        gscore.append(jnp.maximum(hi1, hi2) + jnp.maximum(jnp.minimum(hi1, hi2), jnp.maximum(lo1, lo2)))
    best = jnp.zeros(gscore[0].shape, jnp.int32)
    top = gscore[0]
    for g in range(1, N_GROUPS):
        upd = gscore[g] > top
        best = jnp.where(upd, g, best)
        top = jnp.where(upd, gscore[g], top)

    def pick(rows, j):
        out = rows[(N_GROUPS - 1) * epg + j]
        for g in range(N_GROUPS - 2, -1, -1):
            out = jnp.where(best == g, rows[g * epg + j], out)
        return out

    bv = [pick(brow, j) for j in range(epg)]
    sv = [pick(srow, j) for j in range(epg)]
    chosen = []
    for j in range(epg):
        rank = jnp.zeros(best.shape, jnp.int32)
        for i in range(epg):
            if i == j:
                continue
            ahead = (bv[i] > bv[j]) if i > j else (bv[i] >= bv[j])
            rank = rank + ahead.astype(jnp.int32)
        chosen.append(rank < 2)
    den = None
    for j in range(epg):
        term = jnp.where(chosen[j], sv[j], 0.0)
        den = term if den is None else den + term
    for g in range(N_GROUPS):
        for j in range(epg):
            wgt = jnp.where(jnp.logical_and(chosen[j], best == g), sv[j] / den, 0.0)
            ct_ref[g * epg + j:g * epg + j + 1, :] = wgt


def _router(xf, router_w, router_b, tm=2048):
    t = xf.shape[0]
    tm = min(tm, t)
    return pl.pallas_call(
        _router_kernel,
        out_shape=jax.ShapeDtypeStruct((N_EXPERTS, t), F32),
        grid=(t // tm,),
        in_specs=[pl.BlockSpec((tm, D_MODEL), lambda i: (i, 0)),
                  pl.BlockSpec((D_MODEL, LANES), lambda i: (0, 0)),
                  pl.BlockSpec((N_EXPERTS, 1), lambda i: (0, 0))],
        out_specs=pl.BlockSpec((N_EXPERTS, tm), lambda i: (0, i)),
        compiler_params=_params(("arbitrary",)),
        name="router",
    )(xf, jnp.pad(router_w.astype(F32), ((0, 0), (0, LANES - N_EXPERTS))), router_b.astype(F32)[:, None])


def _moe_kernel(xb_ref, wg_ref, wu_ref, wd_ref, c_ref, x_ref, lg_ref, lb_ref, xo_ref, xbo_ref, acc_ref):
    e = pl.program_id(1)

    @pl.when(e == 0)
    def _():
        acc_ref[...] = jnp.zeros_like(acc_ref)

    xb = xb_ref[...]
    hmid = jax.nn.silu(_dot(xb, wg_ref[0])) * _dot(xb, wu_ref[0])
    comb = c_ref[...]
    lane = lax.broadcasted_iota(jnp.int32, comb.shape, 1)
    wgt = jnp.sum(jnp.where(lane == e, comb, 0.0), axis=1, keepdims=True)
    acc_ref[...] += wgt * _dot(hmid.astype(BF16), wd_ref[0])

    @pl.when(e == pl.num_programs(1) - 1)
    def _():
        xn = _layer_norm(ALPHA * x_ref[...] + acc_ref[...], lg_ref[...], lb_ref[...])
        xo_ref[...] = xn
        xbo_ref[...] = xn.astype(BF16)


def _moe(xb, wg, wu, wd, comb, xf, ln_g, ln_b, tm=1024):
    t = xf.shape[0]
    tm = min(tm, t)
    tile = pl.BlockSpec((tm, D_MODEL), lambda i, e: (i, 0))
    row = pl.BlockSpec((1, D_MODEL), lambda i, e: (0, 0))
    return pl.pallas_call(
        _moe_kernel,
        out_shape=(jax.ShapeDtypeStruct((t, D_MODEL), F32), jax.ShapeDtypeStruct((t, D_MODEL), BF16)),
        grid=(t // tm, N_EXPERTS),
        in_specs=[tile,
                  pl.BlockSpec((1, D_MODEL, D_EXPERT), lambda i, e: (e, 0, 0)),
                  pl.BlockSpec((1, D_MODEL, D_EXPERT), lambda i, e: (e, 0, 0)),
                  pl.BlockSpec((1, D_EXPERT, D_MODEL), lambda i, e: (e, 0, 0)),
                  pl.BlockSpec((tm, N_EXPERTS), lambda i, e: (i, 0)),
                  tile, row, row],
        out_specs=(tile, tile),
        scratch_shapes=[pltpu.VMEM((tm, D_MODEL), F32)],
        compiler_params=_params(("arbitrary", "arbitrary")),
        name="moe_ln",
    )(xb, wg, wu, wd, comb, xf, ln_g.astype(F32)[None, :], ln_b.astype(F32)[None, :])


def kernel(x, positions, w_in, gdn_conv_w, gdn_a_log, gdn_dt_bias, gdn_norm_g, hgrn_lb, hgrn_norm_g, merge_b,
           w_branch, w_out, ln1_g, ln1_b, router_w, router_b, moe_w_gate, moe_w_up, moe_w_down, ln2_g, ln2_b):
    b, s, d = x.shape
    t = b * s
    assert d == D_MODEL and w_in.shape[-1] == N_MAIN + SMALL_WIDTH

    w_main = jnp.concatenate([w_in[:, :, :SMALL_START], w_in[:, :, SMALL_START + SMALL_WIDTH:]], axis=-1).astype(BF16)
    w_small = jnp.pad(w_in[:, :, SMALL_START:SMALL_START + SMALL_WIDTH], ((0, 0), (0, 0), (0, LANES - SMALL_WIDTH))).astype(BF16)
    wb = w_branch.astype(BF16)
    wo = w_out.astype(BF16)
    wg = moe_w_gate.astype(BF16)
    wu = moe_w_up.astype(BF16)
    wd = moe_w_down.astype(BF16)

    cos, sin = _rope_tables(positions)
    cos3 = cos.reshape(b, s, HEAD_K)
    sin3 = sin.reshape(b, s, HEAD_K)

    xf = x.reshape(t, d).astype(F32)
    xb = xf.astype(BF16)
    for l in range(DEPTH):
        proj = _in_proj(xb, w_main[l], BF16, tm=2048, tn=768)
        small = _in_proj(xb, w_small[l], F32, tm=2048, tn=LANES)
        proj3 = proj.reshape(b, s, N_MAIN)
        y_ret = _retention(proj3, cos3, sin3)
        y_gdn = _gdn(proj3, small.reshape(b, s, LANES), gdn_conv_w[l], gdn_a_log[l], gdn_dt_bias[l], gdn_norm_g[l])
        y_hg = _hgrn2(proj3, hgrn_lb, hgrn_norm_g[l], l)
        xf, xb = _merge(y_ret.reshape(t, V_WIDTH), y_gdn.reshape(t, V_WIDTH), y_hg.reshape(t, V_WIDTH), proj,
                        merge_b[l], wb[l], wo[l], xf, ln1_g[l], ln1_b[l])
        comb = _router(xf, router_w, router_b).T
        xf, xb = _moe(xb, wg[l], wu[l], wd[l], comb, xf, ln2_g[l], ln2_b[l])
    return xf.reshape(b, s, d).astype(x.dtype)
```

```python
import functools
import math

import jax
import jax.numpy as jnp
from jax import lax
from jax.experimental import pallas as pl
from jax.experimental.pallas import tpu as pltpu
from jax.experimental.pallas import tpu_sc as plsc

F32 = jnp.float32
BF16 = jnp.bfloat16

D_MODEL = 1024
DEPTH = 4
N_HEADS = 4
HEAD_V = 256
HEAD_K = 128
QK_WIDTH = N_HEADS * HEAD_K
V_WIDTH = N_HEADS * HEAD_V
N_BRANCH = 3
ROPE_BASE = 10000.0
RET_DECAY_OFFSET = 5.0
N_EXPERTS = 16
N_GROUPS = 4
EXPERTS_PER_GROUP = N_EXPERTS // N_GROUPS
D_EXPERT = D_MODEL // 2
ALPHA = (2 * DEPTH) ** 0.25
LN_EPS = 1e-5
NORM_EPS = 1e-6

SMALL_START = 2 * QK_WIDTH + 2 * V_WIDTH + (2 * QK_WIDTH + V_WIDTH) + V_WIDTH
SMALL_WIDTH = 2 * N_HEADS
N_MAIN = 12288
COL_RQ, COL_RK, COL_RV, COL_RG = 0, 512, 1024, 2048
COL_GQ, COL_GK, COL_GV, COL_GZ = 3072, 3584, 4096, 5120
COL_HQ, COL_HF, COL_HI, COL_HG = 6144, 6656, 7168, 8192
COL_MG = 9216

LANES = 128
SUBLANES = 8
VMEM_LIMIT = 56 * 1024 * 1024

RET_CHUNK = 256
CHUNK = 64
STEP_TOKENS = 256
HGRN_SUB = 16
EXPERT_TILE = 512
SC_CORES = 2
SC_SUBCORES = 16
SC_WINDOW = 128


def _dot(a, b):
    return jnp.dot(a, b, preferred_element_type=F32)


def _dot_nt(a, b):
    return lax.dot_general(a, b, (((1,), (1,)), ((), ())), preferred_element_type=F32)


def _dot_tn(a, b):
    return lax.dot_general(a, b, (((0,), (0,)), ((), ())), preferred_element_type=F32)


def _cumsum_rows(x, seg):
    assert seg & (seg - 1) == 0
    pos = lax.broadcasted_iota(jnp.int32, x.shape, 0) & (seg - 1)
    s = 1
    while s < seg:
        x = x + jnp.where(pos >= s, pltpu.roll(x, s, axis=0), 0.0)
        s *= 2
    return x


def _pack_pair(a, b):
    ua = lax.bitcast_convert_type(a.astype(BF16).astype(F32), jnp.int32)
    ub = lax.bitcast_convert_type(b.astype(BF16).astype(F32), jnp.int32)
    return lax.shift_right_logical(ua, 16) | (ub & jnp.int32(-65536))


def _unpack_pair(w):
    a = lax.bitcast_convert_type(lax.shift_left(w, 16), F32)
    b = lax.bitcast_convert_type(w & jnp.int32(-65536), F32)
    return a, b


def _pack_row(x):
    q = D_MODEL // 4
    return _pack_pair(x[:, :q], x[:, q:2 * q]), _pack_pair(x[:, 2 * q:3 * q], x[:, 3 * q:])


def _unpack_row(pa, pb):
    return jnp.concatenate(_unpack_pair(pa) + _unpack_pair(pb), axis=1)


def _layer_norm(z, g, b):
    mu = jnp.mean(z, axis=-1, keepdims=True)
    zc = z - mu
    var = jnp.mean(zc * zc, axis=-1, keepdims=True)
    return zc * lax.rsqrt(var + LN_EPS) * g + b


def _params(sem):
    return pltpu.CompilerParams(dimension_semantics=sem, vmem_limit_bytes=VMEM_LIMIT)


def _rope_kernel(pos_ref, invf_ref, cos_ref, sin_ref):
    ang = pos_ref[...].astype(F32) * invf_ref[...]
    lane = lax.broadcasted_iota(jnp.int32, ang.shape, 1)
    cos_ref[...] = jnp.cos(ang)
    s = jnp.sin(ang)
    sin_ref[...] = jnp.where(lane < HEAD_K // 2, -s, s)


def _rope_tables(positions):
    t = positions.size
    half = HEAD_K // 2
    inv_freq = 1.0 / (ROPE_BASE ** jnp.linspace(0.0, 1.0, half, dtype=F32))
    invf = jnp.concatenate([inv_freq, inv_freq])[None, :]
    ts = min(t, 2048)
    return pl.pallas_call(
        _rope_kernel,
        out_shape=(jax.ShapeDtypeStruct((t, HEAD_K), F32), jax.ShapeDtypeStruct((t, HEAD_K), F32)),
        grid=(t // ts,),
        in_specs=[pl.BlockSpec((ts, 1), lambda i: (i, 0)), pl.BlockSpec((1, HEAD_K), lambda i: (0, 0))],
        out_specs=(pl.BlockSpec((ts, HEAD_K), lambda i: (i, 0)), pl.BlockSpec((ts, HEAD_K), lambda i: (i, 0))),
        compiler_params=_params(("arbitrary",)),
        name="rope_tables",
    )(positions.reshape(t, 1), invf)


def _matmul_kernel(x_ref, w_ref, o_ref):
    o_ref[...] = _dot(x_ref[...], w_ref[...]).astype(o_ref.dtype)


def _in_proj(xb, w, out_dtype, tm, tn):
    t, d = xb.shape
    n = w.shape[1]
    tm = min(tm, t)
    tn = min(tn, n)
    return pl.pallas_call(
        _matmul_kernel,
        out_shape=jax.ShapeDtypeStruct((t, n), out_dtype),
        grid=(t // tm, n // tn),
        in_specs=[pl.BlockSpec((tm, d), lambda i, j: (i, 0)), pl.BlockSpec((d, tn), lambda i, j: (0, j))],
        out_specs=pl.BlockSpec((tm, tn), lambda i, j: (i, j)),
        compiler_params=_params(("arbitrary", "arbitrary")),
        name="in_proj",
    )(xb, w)


def _retention_kernel(q_ref, k_ref, v_ref, g_ref, cos_ref, sin_ref, o_ref, state_ref, *, chunk):
    @pl.when(pl.program_id(1) == 0)
    def _():
        state_ref[...] = jnp.zeros_like(state_ref)

    c = chunk
    cos = cos_ref[0]
    sin = sin_ref[0]
    t_col = lax.broadcasted_iota(jnp.int32, (c, 1), 0).astype(F32)
    rel = (lax.broadcasted_iota(jnp.int32, (c, c), 0) - lax.broadcasted_iota(jnp.int32, (c, c), 1)).astype(F32)
    for h in range(N_HEADS):
        lg = math.log1p(-(2.0 ** (-RET_DECAY_OFFSET - h)))
        q = q_ref[0, :, h * HEAD_K:(h + 1) * HEAD_K].astype(F32)
        k = k_ref[0, :, h * HEAD_K:(h + 1) * HEAD_K].astype(F32)
        q = q * cos + pltpu.roll(q, HEAD_K // 2, axis=1) * sin
        k = (k * cos + pltpu.roll(k, HEAD_K // 2, axis=1) * sin) * (HEAD_K ** -0.5)
        v = v_ref[0, :, h * HEAD_V:(h + 1) * HEAD_V]
        intra = jnp.where(rel >= 0, jnp.exp(lg * rel), 0.0)
        scores = _dot_nt(q.astype(BF16), k.astype(BF16)) * intra
        st = state_ref[h]
        out = _dot(scores.astype(BF16), v) + _dot((q * jnp.exp(lg * (t_col + 1.0))).astype(BF16), st.astype(BF16))
        state_ref[h] = math.exp(lg * c) * st + _dot_tn((k * jnp.exp(lg * (c - 1.0 - t_col))).astype(BF16), v)
        mu = jnp.mean(out, axis=-1, keepdims=True)
        oc = out - mu
        var = jnp.mean(oc * oc, axis=-1, keepdims=True)
        gate = g_ref[0, :, h * HEAD_V:(h + 1) * HEAD_V].astype(F32)
        o_ref[0, :, h * HEAD_V:(h + 1) * HEAD_V] = (jax.nn.silu(gate) * oc * lax.rsqrt(var + NORM_EPS)).astype(o_ref.dtype)


def _retention(proj3, cos3, sin3):
    b, s, _ = proj3.shape
    c = min(RET_CHUNK, s)
    qk = lambda j: pl.BlockSpec((1, c, QK_WIDTH), lambda bi, n: (bi, n, j))
    vv = lambda j: pl.BlockSpec((1, c, V_WIDTH), lambda bi, n: (bi, n, j))
    tab = pl.BlockSpec((1, c, HEAD_K), lambda bi, n: (bi, n, 0))
    return pl.pallas_call(
        functools.partial(_retention_kernel, chunk=c),
        out_shape=jax.ShapeDtypeStruct((b, s, V_WIDTH), BF16),
        grid=(b, s // c),
        in_specs=[qk(COL_RQ // QK_WIDTH), qk(COL_RK // QK_WIDTH), vv(COL_RV // V_WIDTH), vv(COL_RG // V_WIDTH), tab, tab],
        out_specs=pl.BlockSpec((1, c, V_WIDTH), lambda bi, n: (bi, n, 0)),
        scratch_shapes=[pltpu.VMEM((N_HEADS, HEAD_K, HEAD_V), F32)],
        compiler_params=_params(("arbitrary", "arbitrary")),
        name="retention",
    )(proj3, proj3, proj3, proj3, cos3, sin3)


def _conv_silu(x, tail_ref, w):
    c = x.shape[0]
    prev = tail_ref[...]
    row8 = lax.broadcasted_iota(jnp.int32, (SUBLANES, 1), 0)
    y = w[3:4] * x
    for j in (1, 2, 3):
        xs = pltpu.roll(x, j, axis=0)
        head = jnp.where(row8 >= j, xs[:SUBLANES], pltpu.roll(prev, j, axis=0))
        xs = jnp.concatenate([head, xs[SUBLANES:]], axis=0)
        y = y + w[3 - j:4 - j] * xs
    tail_ref[...] = x[c - SUBLANES:]
    return y * jax.nn.sigmoid(y)


def _unit_lower_inverses(mats, order):
    n = mats[0].shape[0]
    eye = (lax.broadcasted_iota(jnp.int32, (n, n), 0) == lax.broadcasted_iota(jnp.int32, (n, n), 1)).astype(F32)
    ps = [-a for a in mats]
    invs = [eye + p for p in ps]
    k = 1
    while 2 * k < order:
        pbs = [p.astype(BF16) for p in ps]
        ps = [_dot(pb, pb) for pb in pbs]
        invs = [inv + _dot(inv.astype(BF16), p.astype(BF16)) for inv, p in zip(invs, ps)]
        k *= 2
    return invs


def _stack_heads(x, rows, width):
    return jnp.concatenate([x[rows, h * width:(h + 1) * width] for h in range(N_HEADS)], axis=0)


def _gdn_kernel(qp_ref, kp_ref, vp_ref, z_ref, sm_ref, cwq_ref, cwk_ref, cwv_ref, alog_ref, dtb_ref, gn_ref,
                o_ref, state_ref, tq_ref, tk_ref, tv_ref, qs_ref, os_ref, ms_ref, ns_ref, *, n_chunks):
    @pl.when(pl.program_id(1) == 0)
    def _():
        state_ref[...] = jnp.zeros_like(state_ref)
        tq_ref[...] = jnp.zeros_like(tq_ref)
        tk_ref[...] = jnp.zeros_like(tk_ref)
        tv_ref[...] = jnp.zeros_like(tv_ref)

    c = CHUNK
    hc = N_HEADS * c
    q_all = _conv_silu(qp_ref[0].astype(F32), tq_ref, cwq_ref[...])
    k_all = _conv_silu(kp_ref[0].astype(F32), tk_ref, cwk_ref[...])
    v_all = _conv_silu(vp_ref[0].astype(F32), tv_ref, cwv_ref[...])
    sm = sm_ref[0]
    beta_all = jax.nn.sigmoid(sm)
    g_all = -jnp.exp(alog_ref[...]) * jax.nn.softplus(sm + dtb_ref[...])
    gc_all = _cumsum_rows(g_all, c)
    gc_t = gc_all.T
    ri = lax.broadcasted_iota(jnp.int32, (hc, hc), 0)
    ci = lax.broadcasted_iota(jnp.int32, (hc, hc), 1)
    shift = c.bit_length() - 1
    lower = jnp.logical_and(ri >> shift == ci >> shift, ri >= ci)
    diag = ri == ci
    row_head = lax.broadcasted_iota(jnp.int32, (hc, 1), 0) >> shift
    col = lambda x, rows, j: jnp.concatenate([x[rows, j + h:j + h + 1] for h in range(N_HEADS)], axis=0)

    pre = []
    for j in range(n_chunks):
        rows = slice(j * c, (j + 1) * c)
        kraw = _stack_heads(k_all, rows, HEAD_K)
        qraw = _stack_heads(q_all, rows, HEAD_K)
        kn = kraw * lax.rsqrt(jnp.sum(kraw * kraw, axis=-1, keepdims=True) + NORM_EPS)
        qn = qraw * lax.rsqrt(jnp.sum(qraw * qraw, axis=-1, keepdims=True) + NORM_EPS) * (HEAD_K ** -0.5)
        beta = col(beta_all, rows, 0)
        gc = col(gc_all, rows, N_HEADS)
        gr = jnp.concatenate([gc_t[N_HEADS + h:N_HEADS + h + 1, rows] for h in range(N_HEADS)], axis=1)
        last = slice((j + 1) * c - 1, (j + 1) * c)
        g_last = [gc_all[last, N_HEADS + h:N_HEADS + h + 1] for h in range(N_HEADS)]
        gl = jnp.concatenate([jnp.broadcast_to(g, (c, 1)) for g in g_last], axis=0)
        pre.append(dict(kn=kn, qn=qn, kb=kn * beta, vb=_stack_heads(v_all, rows, HEAD_V) * beta, gc=gc, gr=gr,
                        eg=jnp.exp(gc), gl=gl, egl=[jnp.exp(g) for g in g_last]))
    decays = [jnp.where(lower, jnp.exp(p["gc"] - p["gr"]), 0.0) for p in pre]
    kqs = [_dot_nt(jnp.concatenate([p["kb"], p["qn"]], axis=0).astype(BF16), p["kn"].astype(BF16)) for p in pre]
    tms = _unit_lower_inverses([jnp.where(diag, 0.0, kq[:hc] * d) for kq, d in zip(kqs, decays)], c)
    uws = [_dot(tm.astype(BF16), jnp.concatenate([p["vb"], p["kb"] * p["eg"]], axis=1).astype(BF16)).astype(BF16)
           for tm, p in zip(tms, pre)]
    aos = [_dot((kq[hc:] * d).astype(BF16), uw) for kq, d, uw in zip(kqs, decays, uws)]
    for j, (p, uw, ao) in enumerate(zip(pre, uws, aos)):
        kd = p["kn"] * jnp.exp(p["gl"] - p["gc"])
        kd_wide = jnp.concatenate([jnp.where(row_head == h, kd, 0.0) for h in range(N_HEADS)], axis=1)
        nm = _dot_tn(kd_wide.astype(BF16), uw)
        qs_ref[j] = (p["qn"] * p["eg"] - ao[:, HEAD_V:]).astype(BF16)
        os_ref[j] = ao[:, :HEAD_V]
        ns_ref[j] = nm[:, :HEAD_V]
        ms_ref[j] = nm[:, HEAD_V:].astype(BF16)

    for j in range(n_chunks):
        sts = [state_ref[h] for h in range(N_HEADS)]
        rrs = [_dot(jnp.concatenate([qs_ref[j, h * c:(h + 1) * c], ms_ref[j, h * HEAD_K:(h + 1) * HEAD_K]], axis=0),
                    sts[h].astype(BF16)) for h in range(N_HEADS)]
        for h in range(N_HEADS):
            state_ref[h] = pre[j]["egl"][h] * sts[h] - rrs[h][c:] + ns_ref[j, h * HEAD_K:(h + 1) * HEAD_K]
        for h in range(N_HEADS):
            out = rrs[h][:c] + os_ref[j, h * c:(h + 1) * c]
            on = out * lax.rsqrt(jnp.mean(out * out, axis=-1, keepdims=True) + NORM_EPS) * gn_ref[...]
            z = z_ref[0, j * c:(j + 1) * c, h * HEAD_V:(h + 1) * HEAD_V].astype(F32)
            o_ref[0, j * c:(j + 1) * c, h * HEAD_V:(h + 1) * HEAD_V] = (jax.nn.silu(z) * on).astype(o_ref.dtype)


def _lane_row(vals, start):
    return jnp.zeros((1, LANES), F32).at[0, start:start + vals.shape[0]].set(vals.astype(F32))


def _gdn(proj3, small3, conv_w, a_log, dt_bias, norm_g):
    b, s, _ = proj3.shape
    c = min(STEP_TOKENS, s)
    g = c // CHUNK
    hc = N_HEADS * CHUNK
    qk = lambda j: pl.BlockSpec((1, c, QK_WIDTH), lambda bi, n: (bi, n, j))
    vv = lambda j: pl.BlockSpec((1, c, V_WIDTH), lambda bi, n: (bi, n, j))
    cw = lambda wd, j: pl.BlockSpec((4, wd), lambda bi, n: (0, j))
    row = lambda wd: pl.BlockSpec((1, wd), lambda bi, n: (0, 0))
    conv_w = conv_w.astype(F32)
    return pl.pallas_call(
        functools.partial(_gdn_kernel, n_chunks=g),
        out_shape=jax.ShapeDtypeStruct((b, s, V_WIDTH), BF16),
        grid=(b, s // c),
        in_specs=[qk(COL_GQ // QK_WIDTH), qk(COL_GK // QK_WIDTH), vv(COL_GV // V_WIDTH), vv(COL_GZ // V_WIDTH),
                  pl.BlockSpec((1, c, LANES), lambda bi, n: (bi, n, 0)),
                  cw(QK_WIDTH, 0), cw(QK_WIDTH, 1), cw(V_WIDTH, 1),
                  row(LANES), row(LANES), row(HEAD_V)],
        out_specs=pl.BlockSpec((1, c, V_WIDTH), lambda bi, n: (bi, n, 0)),
        scratch_shapes=[pltpu.VMEM((N_HEADS, HEAD_K, HEAD_V), F32),
                        pltpu.VMEM((SUBLANES, QK_WIDTH), F32), pltpu.VMEM((SUBLANES, QK_WIDTH), F32),
                        pltpu.VMEM((SUBLANES, V_WIDTH), F32),
                        pltpu.VMEM((g, hc, HEAD_K), BF16), pltpu.VMEM((g, hc, HEAD_V), F32),
                        pltpu.VMEM((g, N_HEADS * HEAD_K, HEAD_K), BF16), pltpu.VMEM((g, N_HEADS * HEAD_K, HEAD_V), F32)],
        compiler_params=_params(("arbitrary", "arbitrary")),
        name="gated_delta",
    )(proj3, proj3, proj3, proj3, small3, conv_w, conv_w, conv_w,
      _lane_row(a_log, N_HEADS), _lane_row(dt_bias, N_HEADS), norm_g.astype(F32)[None, :])


def _hgrn2_kernel(q_ref, f_ref, i_ref, g_ref, lbp_ref, gn_ref, o_ref, state_ref, *, n_chunks, layer):
    @pl.when(pl.program_id(1) == 0)
    def _():
        state_ref[...] = jnp.zeros_like(state_ref)

    c = CHUNK
    p = lbp_ref[...]
    e = jnp.exp(p - jnp.max(p, axis=0, keepdims=True))
    sm = e / jnp.sum(e, axis=0, keepdims=True)
    lb = jnp.zeros((1, QK_WIDTH), F32)
    for i in range(1, layer + 1):
        lb = lb + sm[i:i + 1]
    hf = f_ref[0].astype(F32)
    log_f = jnp.logaddexp(jax.nn.log_sigmoid(hf), jnp.log(lb) + jax.nn.log_sigmoid(-hf))
    k_all = (1.0 - lb) * jax.nn.sigmoid(-hf)
    q_all = q_ref[0].astype(F32) * (HEAD_K ** -0.5)
    gcum = _cumsum_rows(log_f, c)
    sub = HGRN_SUB
    nsub = c // sub
    sub_shift = sub.bit_length() - 1
    lane = lax.broadcasted_iota(jnp.int32, (sub, c), 1)
    srow = lax.broadcasted_iota(jnp.int32, (sub, 1), 0)
    crow = lax.broadcasted_iota(jnp.int32, (c, 1), 0)
    cblk = crow >> sub_shift
    for j in range(n_chunks):
        rows = slice(j * c, (j + 1) * c)
        for h in range(N_HEADS):
            g = gcum[rows, h * HEAD_K:(h + 1) * HEAD_K]
            q = q_all[rows, h * HEAD_K:(h + 1) * HEAD_K]
            k = k_all[rows, h * HEAD_K:(h + 1) * HEAD_K]
            v = i_ref[0, rows, h * HEAD_V:(h + 1) * HEAD_V]
            refs = [g[i * sub - 1:i * sub] for i in range(1, nsub)]
            ref_rows = jnp.concatenate([jnp.zeros((sub, HEAD_K), F32)] + [jnp.broadcast_to(r, (sub, HEAD_K)) for r in refs], axis=0)
            qt = q * jnp.exp(g - ref_rows)
            q_wide = jnp.concatenate([jnp.where(cblk == i, qt, 0.0) for i in range(1, nsub)], axis=1)
            k_wide = jnp.concatenate([jnp.where(crow < i * sub, k * jnp.exp(jnp.minimum(r - g, 0.0)), 0.0)
                                      for i, r in zip(range(1, nsub), refs)], axis=1)
            off_t = _dot_nt(k_wide.astype(BF16), q_wide.astype(BF16))
            blocks = []
            for i in range(nsub):
                r0 = i * sub
                gb = g[r0:r0 + sub]
                qb = q[r0:r0 + sub]
                kb = k[r0:r0 + sub]
                acc_t = jnp.zeros((sub, c), F32)
                for t in range(sub):
                    w = kb * jnp.exp(gb[t:t + 1] - gb) * qb[t:t + 1]
                    col = jnp.where(srow <= t, jnp.sum(w, axis=1, keepdims=True), 0.0)
                    acc_t = jnp.where((lane & (sub - 1)) == t, col, acc_t)
                blocks.append(jnp.where((lane >> sub_shift) == i, acc_t, off_t[r0:r0 + sub]))
            attn_t = jnp.concatenate(blocks, axis=0)
            st = state_ref[h]
            out = _dot_tn(attn_t.astype(BF16), v) + _dot_nt((q * jnp.exp(g)).astype(BF16), st.astype(BF16))
            g_last = g[c - 1:c]
            state_ref[h] = st * jnp.exp(g_last) + _dot_tn(v, (k * jnp.exp(g_last - g)).astype(BF16))
            on = out * lax.rsqrt(jnp.mean(out * out, axis=-1, keepdims=True) + NORM_EPS) * gn_ref[...]
            gate = g_ref[0, rows, h * HEAD_V:(h + 1) * HEAD_V].astype(F32)
            o_ref[0, rows, h * HEAD_V:(h + 1) * HEAD_V] = (jax.nn.sigmoid(gate) * on).astype(o_ref.dtype)


def _hgrn2(proj3, lb_param, norm_g, layer):
    b, s, _ = proj3.shape
    c = min(STEP_TOKENS, s)
    qk = lambda j: pl.BlockSpec((1, c, QK_WIDTH), lambda bi, n: (bi, n, j))
    vv = lambda j: pl.BlockSpec((1, c, V_WIDTH), lambda bi, n: (bi, n, j))
    return pl.pallas_call(
        functools.partial(_hgrn2_kernel, n_chunks=c // CHUNK, layer=layer),
        out_shape=jax.ShapeDtypeStruct((b, s, V_WIDTH), BF16),
        grid=(b, s // c),
        in_specs=[qk(COL_HQ // QK_WIDTH), qk(COL_HF // QK_WIDTH), vv(COL_HI // V_WIDTH), vv(COL_HG // V_WIDTH),
                  pl.BlockSpec((DEPTH, QK_WIDTH), lambda bi, n: (0, 0)),
                  pl.BlockSpec((1, HEAD_V), lambda bi, n: (0, 0))],
        out_specs=pl.BlockSpec((1, c, V_WIDTH), lambda bi, n: (bi, n, 0)),
        scratch_shapes=[pltpu.VMEM((N_HEADS, HEAD_V, HEAD_K), F32)],
        compiler_params=_params(("arbitrary", "arbitrary")),
        name="hgrn2",
    )(proj3, proj3, proj3, proj3, lb_param.astype(F32), norm_g.astype(F32)[None, :])


def _merge_kernel(yr_ref, yg_ref, yh_ref, m0_ref, m1_ref, m2_ref, mb_ref, wb_ref, wo_ref, x_ref, lg_ref, lb_ref,
                  xo_ref, pa_ref, pb_ref):
    merged = None
    for n, (y_ref, m_ref) in enumerate(((yr_ref, m0_ref), (yg_ref, m1_ref), (yh_ref, m2_ref))):
        gate = jax.nn.sigmoid(m_ref[...].astype(F32) + mb_ref[:, n * D_MODEL:(n + 1) * D_MODEL])
        term = gate * _dot(y_ref[...], wb_ref[n])
        merged = term if merged is None else merged + term
    h = _dot(merged.astype(BF16), wo_ref[...])
    xn = _layer_norm(ALPHA * x_ref[...] + h, lg_ref[...], lb_ref[...])
    xo_ref[...] = xn
    pa_ref[...], pb_ref[...] = _pack_row(xn)


def _merge(y_ret, y_gdn, y_hg, proj, merge_b, wb, wo, xf, ln_g, ln_b, tm=512):
    t = xf.shape[0]
    tm = min(tm, t)
    tile = lambda j: pl.BlockSpec((tm, D_MODEL), lambda i: (i, j))
    row = lambda wd: pl.BlockSpec((1, wd), lambda i: (0, 0))
    packed = pl.BlockSpec((tm, D_MODEL // 4), lambda i: (i, 0))
    packed_shape = jax.ShapeDtypeStruct((t, D_MODEL // 4), jnp.int32)
    return pl.pallas_call(
        _merge_kernel,
        out_shape=(jax.ShapeDtypeStruct((t, D_MODEL), F32), packed_shape, packed_shape),
        grid=(t // tm,),
        in_specs=[tile(0), tile(0), tile(0),
                  tile(COL_MG // D_MODEL), tile(COL_MG // D_MODEL + 1), tile(COL_MG // D_MODEL + 2),
                  row(N_BRANCH * D_MODEL),
                  pl.BlockSpec((N_BRANCH, V_WIDTH, D_MODEL), lambda i: (0, 0, 0)),
                  pl.BlockSpec((D_MODEL, D_MODEL), lambda i: (0, 0)),
                  tile(0), row(D_MODEL), row(D_MODEL)],
        out_specs=(tile(0), packed, packed),
        compiler_params=_params(("arbitrary",)),
        name="merge_out_ln",
    )(y_ret, y_gdn, y_hg, proj, proj, proj, merge_b.astype(F32)[None, :], wb, wo, xf,
      ln_g.astype(F32)[None, :], ln_b.astype(F32)[None, :])


def _router_kernel(x_ref, rw_ref, rb_ref, ct_ref, rk_ref, cnt_ref, sel_ref, carry_ref):
    @pl.when(pl.program_id(0) == 0)
    def _():
        carry_ref[...] = jnp.zeros_like(carry_ref)

    x = x_ref[...]
    w = rw_ref[...]
    xh = x.astype(BF16)
    xl = (x - xh.astype(F32)).astype(BF16)
    wh = w.astype(BF16)
    wl = (w - wh.astype(F32)).astype(BF16)
    logits = (_dot(xh, wh) + (_dot(xl, wh) + _dot(xh, wl))).T[:N_EXPERTS]
    scores = jax.nn.sigmoid(logits)
    biased = scores + rb_ref[...]
    epg = EXPERTS_PER_GROUP
    brow = [biased[e:e + 1] for e in range(N_EXPERTS)]
    srow = [scores[e:e + 1] for e in range(N_EXPERTS)]
    gscore = []
    for g in range(N_GROUPS):
        a, b, c, d = brow[g * epg:(g + 1) * epg]
        hi1, lo1, hi2, lo2 = jnp.maximum(a, b), jnp.minimum(a, b), jnp.maximum(c, d), jnp.minimum(c, d)
        gscore.append(jnp.maximum(hi1, hi2) + jnp.maximum(jnp.minimum(hi1, hi2), jnp.maximum(lo1, lo2)))
    best = jnp.zeros(gscore[0].shape, jnp.int32)
    top = gscore[0]
    for g in range(1, N_GROUPS):
        upd = gscore[g] > top
        best = jnp.where(upd, g, best)
        top = jnp.where(upd, gscore[g], top)

    def pick(rows, j):
        out = rows[(N_GROUPS - 1) * epg + j]
        for g in range(N_GROUPS - 2, -1, -1):
            out = jnp.where(best == g, rows[g * epg + j], out)
        return out

    bv = [pick(brow, j) for j in range(epg)]
    sv = [pick(srow, j) for j in range(epg)]
    chosen = []
    for j in range(epg):
        rank = jnp.zeros(best.shape, jnp.int32)
        for i in range(epg):
            if i == j:
                continue
            ahead = (bv[i] > bv[j]) if i > j else (bv[i] >= bv[j])
            rank = rank + ahead.astype(jnp.int32)
        chosen.append(rank < 2)
    den = None
    for j in range(epg):
        term = jnp.where(chosen[j], sv[j], 0.0)
        den = term if den is None else den + term
    for g in range(N_GROUPS):
        for j in range(epg):
            e = g * epg + j
            hit = jnp.logical_and(chosen[j], best == g)
            ct_ref[e:e + 1, :] = jnp.where(hit, sv[j] / den, 0.0)
            sel_ref[e:e + 1, :] = jnp.where(hit, 1.0, 0.0)
    sel = sel_ref[...]
    tm = sel.shape[1]
    before = (lax.broadcasted_iota(jnp.int32, (tm, tm), 0) < lax.broadcasted_iota(jnp.int32, (tm, tm), 1))
    carry = carry_ref[:, 0:1]
    rank = _dot(sel.astype(BF16), jnp.where(before, 1.0, 0.0).astype(BF16)) + carry
    rk_ref[...] = jnp.where(sel > 0.0, rank, -1.0).astype(jnp.int32)
    total = jnp.broadcast_to(carry + jnp.sum(sel, axis=1, keepdims=True), carry_ref.shape)
    carry_ref[...] = total
    cnt_ref[...] = total


def _router(xf, router_w, router_b, tm=1024):
    t = xf.shape[0]
    tm = min(tm, t)
    col = pl.BlockSpec((N_EXPERTS, tm), lambda i: (0, i))
    return pl.pallas_call(
        _router_kernel,
        out_shape=(jax.ShapeDtypeStruct((N_EXPERTS, t), F32), jax.ShapeDtypeStruct((N_EXPERTS, t), jnp.int32),
                   jax.ShapeDtypeStruct((N_EXPERTS, LANES), F32)),
        grid=(t // tm,),
        in_specs=[pl.BlockSpec((tm, D_MODEL), lambda i: (i, 0)),
                  pl.BlockSpec((D_MODEL, LANES), lambda i: (0, 0)),
                  pl.BlockSpec((N_EXPERTS, 1), lambda i: (0, 0))],
        out_specs=(col, col, pl.BlockSpec((N_EXPERTS, LANES), lambda i: (0, 0))),
        scratch_shapes=[pltpu.VMEM((N_EXPERTS, tm), F32), pltpu.VMEM((N_EXPERTS, LANES), F32)],
        compiler_params=_params(("arbitrary",)),
        name="router",
    )(xf, jnp.pad(router_w.astype(F32), ((0, 0), (0, LANES - N_EXPERTS))), router_b.astype(F32)[:, None])


def _sc_mesh():
    return plsc.VectorSubcoreMesh(core_axis_name="core", subcore_axis_name="subcore",
                                  num_cores=SC_CORES, num_subcores=SC_SUBCORES)


def _sc_scatter_rows(src, idx0, idx1, n_out):
    n, w = src.shape

    @pl.kernel(out_type=jax.ShapeDtypeStruct((n_out, w), src.dtype), mesh=_sc_mesh(), scratch_types=[],
               name="dispatch_rows")
    def run(x_hbm, i0_hbm, i1_hbm, o_hbm):
        def body(x_vmem, i0_vmem, i1_vmem):
            pltpu.sync_copy(x_vmem, o_hbm.at[i0_vmem.at[0]])
            pltpu.sync_copy(x_vmem, o_hbm.at[i1_vmem.at[0]])

        pltpu.emit_pipeline(
            body, grid=(n // SC_WINDOW,),
            in_specs=[pl.BlockSpec((SC_WINDOW, w), index_map=lambda i: (i, 0)),
                      pl.BlockSpec((1, SC_WINDOW), index_map=lambda i: (0, i)),
                      pl.BlockSpec((1, SC_WINDOW), index_map=lambda i: (0, i))],
            out_specs=[], core_axis_name=("core", "subcore"), dimension_semantics=(pltpu.PARALLEL,),
        )(x_hbm, i0_hbm, i1_hbm)

    return run(src, idx0.reshape(1, n), idx1.reshape(1, n))


def _sc_gather_rows(src, idx):
    n = idx.shape[0]
    w = src.shape[1]

    @pl.kernel(out_type=jax.ShapeDtypeStruct((n, w), src.dtype), mesh=_sc_mesh(), scratch_types=[],
               name="collect_rows")
    def run(x_hbm, i_hbm, o_hbm):
        def body(i_vmem, o_vmem):
            pltpu.sync_copy(x_hbm.at[i_vmem.at[0]], o_vmem)

        pltpu.emit_pipeline(
            body, grid=(n // SC_WINDOW,),
            in_specs=[pl.BlockSpec((1, SC_WINDOW), index_map=lambda i: (0, i))],
            out_specs=[pl.BlockSpec((SC_WINDOW, w), index_map=lambda i: (i, 0))],
            core_axis_name=("core", "subcore"), dimension_semantics=(pltpu.PARALLEL,),
        )(i_hbm, o_hbm)

    return run(src, idx.reshape(1, n))


def _dispatch_plan(rank_t, comb_t, counts):
    tg = EXPERT_TILE
    t = rank_t.shape[1]
    n_tiles = 2 * t // tg + N_EXPERTS
    cnt = counts[:, 0].astype(jnp.int32)
    tiles = (cnt + tg - 1) // tg
    tile_end = jnp.cumsum(tiles)
    seg_start = (tile_end - tiles) * tg
    n_active = tile_end[-1]
    tile_ids = jnp.arange(n_tiles, dtype=jnp.int32)
    tile_expert = jnp.searchsorted(tile_end, jnp.minimum(tile_ids, n_active - 1), side="right").astype(jnp.int32)
    routed = rank_t >= 0
    pos = seg_start[:, None] + rank_t
    order = jnp.cumsum(routed.astype(jnp.int32), axis=0)
    first = jnp.logical_and(routed, order == 1)
    second = jnp.logical_and(routed, order == 2)
    dest0 = jnp.sum(jnp.where(first, pos, 0), axis=0)
    dest1 = jnp.sum(jnp.where(second, pos, 0), axis=0)
    w01 = jnp.stack([jnp.sum(jnp.where(first, comb_t, 0.0), axis=0), jnp.sum(jnp.where(second, comb_t, 0.0), axis=0)], axis=1)
    return dest0, dest1, w01, tile_expert, n_active.reshape(1).astype(jnp.int32), n_tiles * tg


def _expert_kernel(te_ref, na_ref, xa_ref, xb_ref, wg_ref, wu_ref, wd_ref, ya_ref, yb_ref):
    del te_ref
    live = pl.program_id(0) < na_ref[0]

    @pl.when(live)
    def _():
        x = _unpack_row(xa_ref[...], xb_ref[...]).astype(BF16)
        h = jax.nn.silu(_dot(x, wg_ref[0])) * _dot(x, wu_ref[0])
        ya_ref[...], yb_ref[...] = _pack_row(_dot(h.astype(BF16), wd_ref[0]))

    @pl.when(jnp.logical_not(live))
    def _():
        ya_ref[...] = jnp.zeros_like(ya_ref)
        yb_ref[...] = jnp.zeros_like(yb_ref)


def _experts(xsa, xsb, wg, wu, wd, tile_expert, n_active):
    n, q = xsa.shape
    tg = EXPERT_TILE
    rows = pl.BlockSpec((tg, q), lambda i, te, na: (i, 0))
    grid_spec = pltpu.PrefetchScalarGridSpec(
        num_scalar_prefetch=2, grid=(n // tg,),
        in_specs=[rows, rows,
                  pl.BlockSpec((1, D_MODEL, D_EXPERT), lambda i, te, na: (te[i], 0, 0)),
                  pl.BlockSpec((1, D_MODEL, D_EXPERT), lambda i, te, na: (te[i], 0, 0)),
                  pl.BlockSpec((1, D_EXPERT, D_MODEL), lambda i, te, na: (te[i], 0, 0))],
        out_specs=(rows, rows))
    shape = jax.ShapeDtypeStruct((n, q), jnp.int32)
    return pl.pallas_call(
        _expert_kernel, grid_spec=grid_spec, out_shape=(shape, shape),
        compiler_params=_params(("arbitrary",)), name="experts",
    )(tile_expert, n_active, xsa, xsb, wg, wu, wd)


def _combine_kernel(g0a_ref, g0b_ref, g1a_ref, g1b_ref, w_ref, x_ref, lg_ref, lb_ref, xo_ref, xbo_ref):
    y0 = _unpack_row(g0a_ref[...], g0b_ref[...])
    y1 = _unpack_row(g1a_ref[...], g1b_ref[...])
    w = w_ref[...]
    xn = _layer_norm(ALPHA * x_ref[...] + (w[:, 0:1] * y0 + w[:, 1:2] * y1), lg_ref[...], lb_ref[...])
    xo_ref[...] = xn
    xbo_ref[...] = xn.astype(BF16)


def _combine(g0a, g0b, g1a, g1b, w01, xf, ln_g, ln_b, tm=1024):
    t = xf.shape[0]
    tm = min(tm, t)
    tile = pl.BlockSpec((tm, D_MODEL), lambda i: (i, 0))
    packed = pl.BlockSpec((tm, D_MODEL // 4), lambda i: (i, 0))
    row = pl.BlockSpec((1, D_MODEL), lambda i: (0, 0))
    return pl.pallas_call(
        _combine_kernel,
        out_shape=(jax.ShapeDtypeStruct((t, D_MODEL), F32), jax.ShapeDtypeStruct((t, D_MODEL), BF16)),
        grid=(t // tm,),
        in_specs=[packed, packed, packed, packed, pl.BlockSpec((tm, 2), lambda i: (i, 0)), tile, row, row],
        out_specs=(tile, tile),
        compiler_params=_params(("arbitrary",)),
        name="combine_ln",
    )(g0a, g0b, g1a, g1b, w01, xf, ln_g.astype(F32)[None, :], ln_b.astype(F32)[None, :])


def kernel(x, positions, w_in, gdn_conv_w, gdn_a_log, gdn_dt_bias, gdn_norm_g, hgrn_lb, hgrn_norm_g, merge_b,
           w_branch, w_out, ln1_g, ln1_b, router_w, router_b, moe_w_gate, moe_w_up, moe_w_down, ln2_g, ln2_b):
    b, s, d = x.shape
    t = b * s
    assert d == D_MODEL and w_in.shape[-1] == N_MAIN + SMALL_WIDTH

    w_main = jnp.concatenate([w_in[:, :, :SMALL_START], w_in[:, :, SMALL_START + SMALL_WIDTH:]], axis=-1).astype(BF16)
    w_small = jnp.pad(w_in[:, :, SMALL_START:SMALL_START + SMALL_WIDTH], ((0, 0), (0, 0), (0, LANES - SMALL_WIDTH))).astype(BF16)
    wb = w_branch.astype(BF16)
    wo = w_out.astype(BF16)
    wg = moe_w_gate.astype(BF16)
    wu = moe_w_up.astype(BF16)
    wd = moe_w_down.astype(BF16)

    cos, sin = _rope_tables(positions)
    cos3 = cos.reshape(b, s, HEAD_K)
    sin3 = sin.reshape(b, s, HEAD_K)

    xf = x.reshape(t, d).astype(F32)
    xb = xf.astype(BF16)
    for l in range(DEPTH):
        proj = _in_proj(xb, w_main[l], BF16, tm=2048, tn=768)
        small = _in_proj(xb, w_small[l], F32, tm=2048, tn=LANES)
        proj3 = proj.reshape(b, s, N_MAIN)
        y_ret = _retention(proj3, cos3, sin3)
        y_gdn = _gdn(proj3, small.reshape(b, s, LANES), gdn_conv_w[l], gdn_a_log[l], gdn_dt_bias[l], gdn_norm_g[l])
        y_hg = _hgrn2(proj3, hgrn_lb, hgrn_norm_g[l], l)
        xf, pa, pb = _merge(y_ret.reshape(t, V_WIDTH), y_gdn.reshape(t, V_WIDTH), y_hg.reshape(t, V_WIDTH), proj,
                            merge_b[l], wb[l], wo[l], xf, ln1_g[l], ln1_b[l])
        comb_t, rank_t, counts = _router(xf, router_w, router_b)
        dest0, dest1, w01, tile_expert, n_active, n_rows = _dispatch_plan(rank_t, comb_t, counts)
        ysa, ysb = _experts(_sc_scatter_rows(pa, dest0, dest1, n_rows), _sc_scatter_rows(pb, dest0, dest1, n_rows),
                            wg[l], wu[l], wd[l], tile_expert, n_active)
        xf, xb = _combine(_sc_gather_rows(ysa, dest0), _sc_gather_rows(ysb, dest0),
                          _sc_gather_rows(ysa, dest1), _sc_gather_rows(ysb, dest1), w01, xf, ln2_g[l], ln2_b[l])
    return xf.reshape(b, s, d).astype(x.dtype)
```

```python
import functools
import math

import jax
import jax.numpy as jnp
from jax import lax
from jax.experimental import pallas as pl
from jax.experimental.pallas import tpu as pltpu
from jax.experimental.pallas import tpu_sc as plsc

F32 = jnp.float32
BF16 = jnp.bfloat16

D_MODEL = 1024
DEPTH = 4
N_HEADS = 4
HEAD_V = 256
HEAD_K = 128
QK_WIDTH = N_HEADS * HEAD_K
V_WIDTH = N_HEADS * HEAD_V
N_BRANCH = 3
ROPE_BASE = 10000.0
RET_DECAY_OFFSET = 5.0
N_EXPERTS = 16
N_GROUPS = 4
EXPERTS_PER_GROUP = N_EXPERTS // N_GROUPS
D_EXPERT = D_MODEL // 2
ALPHA = (2 * DEPTH) ** 0.25
LN_EPS = 1e-5
NORM_EPS = 1e-6

SMALL_START = 2 * QK_WIDTH + 2 * V_WIDTH + (2 * QK_WIDTH + V_WIDTH) + V_WIDTH
SMALL_WIDTH = 2 * N_HEADS
N_MAIN = 12288
COL_RQ, COL_RK, COL_RV, COL_RG = 0, 512, 1024, 2048
COL_GQ, COL_GK, COL_GV, COL_GZ = 3072, 3584, 4096, 5120
COL_HQ, COL_HF, COL_HI, COL_HG = 6144, 6656, 7168, 8192
COL_MG = 9216

LANES = 128
SUBLANES = 8
VMEM_LIMIT = 56 * 1024 * 1024

RET_CHUNK = 256
CHUNK = 64
STEP_TOKENS = 256
HGRN_SUB = 16
EXPERT_TILE = 512
SC_CORES = 2
SC_SUBCORES = 16
SC_WINDOW = 128


def _dot(a, b):
    return jnp.dot(a, b, preferred_element_type=F32)


def _dot_nt(a, b):
    return lax.dot_general(a, b, (((1,), (1,)), ((), ())), preferred_element_type=F32)


def _dot_tn(a, b):
    return lax.dot_general(a, b, (((0,), (0,)), ((), ())), preferred_element_type=F32)


def _cumsum_rows(x, seg):
    assert seg & (seg - 1) == 0
    pos = lax.broadcasted_iota(jnp.int32, x.shape, 0) & (seg - 1)
    s = 1
    while s < seg:
        x = x + jnp.where(pos >= s, pltpu.roll(x, s, axis=0), 0.0)
        s *= 2
    return x


def _pack_pair(a, b):
    ua = lax.bitcast_convert_type(a.astype(BF16).astype(F32), jnp.int32)
    ub = lax.bitcast_convert_type(b.astype(BF16).astype(F32), jnp.int32)
    return lax.shift_right_logical(ua, 16) | (ub & jnp.int32(-65536))


def _unpack_pair(w):
    a = lax.bitcast_convert_type(lax.shift_left(w, 16), F32)
    b = lax.bitcast_convert_type(w & jnp.int32(-65536), F32)
    return a, b


def _pack_row(x):
    q = D_MODEL // 4
    return _pack_pair(x[:, :q], x[:, q:2 * q]), _pack_pair(x[:, 2 * q:3 * q], x[:, 3 * q:])


def _unpack_row(pa, pb):
    return jnp.concatenate(_unpack_pair(pa) + _unpack_pair(pb), axis=1)


def _layer_norm(z, g, b):
    mu = jnp.mean(z, axis=-1, keepdims=True)
    zc = z - mu
    var = jnp.mean(zc * zc, axis=-1, keepdims=True)
    return zc * lax.rsqrt(var + LN_EPS) * g + b


def _params(sem):
    return pltpu.CompilerParams(dimension_semantics=sem, vmem_limit_bytes=VMEM_LIMIT)


def _rope_kernel(pos_ref, invf_ref, cos_ref, sin_ref):
    ang = pos_ref[...].astype(F32) * invf_ref[...]
    lane = lax.broadcasted_iota(jnp.int32, ang.shape, 1)
    cos_ref[...] = jnp.cos(ang)
    s = jnp.sin(ang)
    sin_ref[...] = jnp.where(lane < HEAD_K // 2, -s, s)


def _rope_tables(positions):
    t = positions.size
    half = HEAD_K // 2
    inv_freq = 1.0 / (ROPE_BASE ** jnp.linspace(0.0, 1.0, half, dtype=F32))
    invf = jnp.concatenate([inv_freq, inv_freq])[None, :]
    ts = min(t, 2048)
    return pl.pallas_call(
        _rope_kernel,
        out_shape=(jax.ShapeDtypeStruct((t, HEAD_K), F32), jax.ShapeDtypeStruct((t, HEAD_K), F32)),
        grid=(t // ts,),
        in_specs=[pl.BlockSpec((ts, 1), lambda i: (i, 0)), pl.BlockSpec((1, HEAD_K), lambda i: (0, 0))],
        out_specs=(pl.BlockSpec((ts, HEAD_K), lambda i: (i, 0)), pl.BlockSpec((ts, HEAD_K), lambda i: (i, 0))),
        compiler_params=_params(("arbitrary",)),
        name="rope_tables",
    )(positions.reshape(t, 1), invf)


def _matmul_kernel(x_ref, w_ref, o_ref):
    o_ref[...] = _dot(x_ref[...], w_ref[...]).astype(o_ref.dtype)


def _in_proj(xb, w, out_dtype, tm, tn):
    t, d = xb.shape
    n = w.shape[1]
    tm = min(tm, t)
    tn = min(tn, n)
    return pl.pallas_call(
        _matmul_kernel,
        out_shape=jax.ShapeDtypeStruct((t, n), out_dtype),
        grid=(t // tm, n // tn),
        in_specs=[pl.BlockSpec((tm, d), lambda i, j: (i, 0)), pl.BlockSpec((d, tn), lambda i, j: (0, j))],
        out_specs=pl.BlockSpec((tm, tn), lambda i, j: (i, j)),
        compiler_params=_params(("arbitrary", "arbitrary")),
        name="in_proj",
    )(xb, w)


def _retention_kernel(q_ref, k_ref, v_ref, g_ref, cos_ref, sin_ref, o_ref, state_ref, intra_ref, qdec_ref, kdec_ref,
                      *, chunk):
    @pl.when(pl.program_id(1) == 0)
    def _():
        state_ref[...] = jnp.zeros_like(state_ref)

    c = chunk
    log_gamma = [math.log1p(-(2.0 ** (-RET_DECAY_OFFSET - h))) for h in range(N_HEADS)]

    @pl.when(jnp.logical_and(pl.program_id(0) == 0, pl.program_id(1) == 0))
    def _():
        t_col = lax.broadcasted_iota(jnp.int32, (c, HEAD_K), 0).astype(F32)
        rel = (lax.broadcasted_iota(jnp.int32, (c, c), 0) - lax.broadcasted_iota(jnp.int32, (c, c), 1)).astype(F32)
        for h, lg in enumerate(log_gamma):
            intra_ref[h] = jnp.where(rel >= 0, jnp.exp(lg * rel), 0.0) * (HEAD_K ** -0.5)
            qdec_ref[h] = jnp.exp(lg * (t_col + 1.0))
            kdec_ref[h] = jnp.exp(lg * (c - 1.0 - t_col)) * (HEAD_K ** -0.5)

    cos = cos_ref[0]
    sin = sin_ref[0]
    for h, lg in enumerate(log_gamma):
        q = q_ref[0, :, h * HEAD_K:(h + 1) * HEAD_K].astype(F32)
        k = k_ref[0, :, h * HEAD_K:(h + 1) * HEAD_K].astype(F32)
        q = q * cos + pltpu.roll(q, HEAD_K // 2, axis=1) * sin
        k = k * cos + pltpu.roll(k, HEAD_K // 2, axis=1) * sin
        v = v_ref[0, :, h * HEAD_V:(h + 1) * HEAD_V]
        scores = _dot_nt(q.astype(BF16), k.astype(BF16)) * intra_ref[h]
        st = state_ref[h]
        out = _dot(scores.astype(BF16), v) + _dot((q * qdec_ref[h]).astype(BF16), st.astype(BF16))
        state_ref[h] = math.exp(lg * c) * st + _dot_tn((k * kdec_ref[h]).astype(BF16), v)
        mu = jnp.mean(out, axis=-1, keepdims=True)
        oc = out - mu
        var = jnp.mean(oc * oc, axis=-1, keepdims=True)
        gate = g_ref[0, :, h * HEAD_V:(h + 1) * HEAD_V].astype(F32)
        o_ref[0, :, h * HEAD_V:(h + 1) * HEAD_V] = (jax.nn.silu(gate) * oc * lax.rsqrt(var + NORM_EPS)).astype(o_ref.dtype)


def _retention(proj3, cos3, sin3):
    b, s, _ = proj3.shape
    c = min(RET_CHUNK, s)
    qk = lambda j: pl.BlockSpec((1, c, QK_WIDTH), lambda bi, n: (bi, n, j))
    vv = lambda j: pl.BlockSpec((1, c, V_WIDTH), lambda bi, n: (bi, n, j))
    tab = pl.BlockSpec((1, c, HEAD_K), lambda bi, n: (bi, n, 0))
    return pl.pallas_call(
        functools.partial(_retention_kernel, chunk=c),
        out_shape=jax.ShapeDtypeStruct((b, s, V_WIDTH), BF16),
        grid=(b, s // c),
        in_specs=[qk(COL_RQ // QK_WIDTH), qk(COL_RK // QK_WIDTH), vv(COL_RV // V_WIDTH), vv(COL_RG // V_WIDTH), tab, tab],
        out_specs=pl.BlockSpec((1, c, V_WIDTH), lambda bi, n: (bi, n, 0)),
        scratch_shapes=[pltpu.VMEM((N_HEADS, HEAD_K, HEAD_V), F32), pltpu.VMEM((N_HEADS, c, c), F32),
                        pltpu.VMEM((N_HEADS, c, HEAD_K), F32), pltpu.VMEM((N_HEADS, c, HEAD_K), F32)],
        compiler_params=_params(("arbitrary", "arbitrary")),
        name="retention",
    )(proj3, proj3, proj3, proj3, cos3, sin3)


def _conv_silu(x, tail_ref, w):
    c = x.shape[0]
    prev = tail_ref[...]
    row8 = lax.broadcasted_iota(jnp.int32, (SUBLANES, 1), 0)
    y = w[3:4] * x
    for j in (1, 2, 3):
        xs = pltpu.roll(x, j, axis=0)
        head = jnp.where(row8 >= j, xs[:SUBLANES], pltpu.roll(prev, j, axis=0))
        xs = jnp.concatenate([head, xs[SUBLANES:]], axis=0)
        y = y + w[3 - j:4 - j] * xs
    tail_ref[...] = x[c - SUBLANES:]
    return y * jax.nn.sigmoid(y)


def _unit_lower_inverses(mats, order):
    n = mats[0].shape[0]
    eye = (lax.broadcasted_iota(jnp.int32, (n, n), 0) == lax.broadcasted_iota(jnp.int32, (n, n), 1)).astype(F32)
    ps = [-a for a in mats]
    invs = [eye + p for p in ps]
    k = 1
    while 2 * k < order:
        pbs = [p.astype(BF16) for p in ps]
        ps = [_dot(pb, pb) for pb in pbs]
        invs = [inv + _dot(inv.astype(BF16), p.astype(BF16)) for inv, p in zip(invs, ps)]
        k *= 2
    return invs


def _stack_heads(x, rows, width):
    return jnp.concatenate([x[rows, h * width:(h + 1) * width] for h in range(N_HEADS)], axis=0)


def _gdn_kernel(qp_ref, kp_ref, vp_ref, z_ref, sm_ref, cwq_ref, cwk_ref, cwv_ref, alog_ref, dtb_ref, gn_ref,
                o_ref, state_ref, tq_ref, tk_ref, tv_ref, qs_ref, os_ref, ms_ref, ns_ref, *, n_chunks):
    @pl.when(pl.program_id(1) == 0)
    def _():
        state_ref[...] = jnp.zeros_like(state_ref)
        tq_ref[...] = jnp.zeros_like(tq_ref)
        tk_ref[...] = jnp.zeros_like(tk_ref)
        tv_ref[...] = jnp.zeros_like(tv_ref)

    c = CHUNK
    hc = N_HEADS * c
    q_all = _conv_silu(qp_ref[0].astype(F32), tq_ref, cwq_ref[...])
    k_all = _conv_silu(kp_ref[0].astype(F32), tk_ref, cwk_ref[...])
    v_all = _conv_silu(vp_ref[0].astype(F32), tv_ref, cwv_ref[...])
    sm = sm_ref[0]
    beta_all = jax.nn.sigmoid(sm)
    g_all = -jnp.exp(alog_ref[...]) * jax.nn.softplus(sm + dtb_ref[...])
    gc_all = _cumsum_rows(g_all, c)
    gc_t = gc_all.T
    ri = lax.broadcasted_iota(jnp.int32, (hc, hc), 0)
    ci = lax.broadcasted_iota(jnp.int32, (hc, hc), 1)
    shift = c.bit_length() - 1
    lower = jnp.logical_and(ri >> shift == ci >> shift, ri >= ci)
    diag = ri == ci
    row_head = lax.broadcasted_iota(jnp.int32, (hc, 1), 0) >> shift
    col = lambda x, rows, j: jnp.concatenate([x[rows, j + h:j + h + 1] for h in range(N_HEADS)], axis=0)

    pre = []
    for j in range(n_chunks):
        rows = slice(j * c, (j + 1) * c)
        kraw = _stack_heads(k_all, rows, HEAD_K)
        qraw = _stack_heads(q_all, rows, HEAD_K)
        kn = kraw * lax.rsqrt(jnp.sum(kraw * kraw, axis=-1, keepdims=True) + NORM_EPS)
        qn = qraw * lax.rsqrt(jnp.sum(qraw * qraw, axis=-1, keepdims=True) + NORM_EPS) * (HEAD_K ** -0.5)
        beta = col(beta_all, rows, 0)
        gc = col(gc_all, rows, N_HEADS)
        gr = jnp.concatenate([gc_t[N_HEADS + h:N_HEADS + h + 1, rows] for h in range(N_HEADS)], axis=1)
        last = slice((j + 1) * c - 1, (j + 1) * c)
        g_last = [gc_all[last, N_HEADS + h:N_HEADS + h + 1] for h in range(N_HEADS)]
        gl = jnp.concatenate([jnp.broadcast_to(g, (c, 1)) for g in g_last], axis=0)
        pre.append(dict(kn=kn, qn=qn, kb=kn * beta, vb=_stack_heads(v_all, rows, HEAD_V) * beta, gc=gc, gr=gr,
                        eg=jnp.exp(gc), gl=gl, egl=[jnp.exp(g) for g in g_last]))
    decays = [jnp.where(lower, jnp.exp(p["gc"] - p["gr"]), 0.0) for p in pre]
    kqs = [_dot_nt(jnp.concatenate([p["kb"], p["qn"]], axis=0).astype(BF16), p["kn"].astype(BF16)) for p in pre]
    tms = _unit_lower_inverses([jnp.where(diag, 0.0, kq[:hc] * d) for kq, d in zip(kqs, decays)], c)
    uws = [_dot(tm.astype(BF16), jnp.concatenate([p["vb"], p["kb"] * p["eg"]], axis=1).astype(BF16)).astype(BF16)
           for tm, p in zip(tms, pre)]
    aos = [_dot((kq[hc:] * d).astype(BF16), uw) for kq, d, uw in zip(kqs, decays, uws)]
    for j, (p, uw, ao) in enumerate(zip(pre, uws, aos)):
        kd = p["kn"] * jnp.exp(p["gl"] - p["gc"])
        kd_wide = jnp.concatenate([jnp.where(row_head == h, kd, 0.0) for h in range(N_HEADS)], axis=1)
        nm = _dot_tn(kd_wide.astype(BF16), uw)
        qs_ref[j] = (p["qn"] * p["eg"] - ao[:, HEAD_V:]).astype(BF16)
        os_ref[j] = ao[:, :HEAD_V]
        ns_ref[j] = nm[:, :HEAD_V]
        ms_ref[j] = nm[:, HEAD_V:].astype(BF16)

    for j in range(n_chunks):
        sts = [state_ref[h] for h in range(N_HEADS)]
        rrs = [_dot(jnp.concatenate([qs_ref[j, h * c:(h + 1) * c], ms_ref[j, h * HEAD_K:(h + 1) * HEAD_K]], axis=0),
                    sts[h].astype(BF16)) for h in range(N_HEADS)]
        for h in range(N_HEADS):
            state_ref[h] = pre[j]["egl"][h] * sts[h] - rrs[h][c:] + ns_ref[j, h * HEAD_K:(h + 1) * HEAD_K]
        for h in range(N_HEADS):
            out = rrs[h][:c] + os_ref[j, h * c:(h + 1) * c]
            on = out * lax.rsqrt(jnp.mean(out * out, axis=-1, keepdims=True) + NORM_EPS) * gn_ref[...]
            z = z_ref[0, j * c:(j + 1) * c, h * HEAD_V:(h + 1) * HEAD_V].astype(F32)
            o_ref[0, j * c:(j + 1) * c, h * HEAD_V:(h + 1) * HEAD_V] = (jax.nn.silu(z) * on).astype(o_ref.dtype)


def _lane_row(vals, start):
    return jnp.zeros((1, LANES), F32).at[0, start:start + vals.shape[0]].set(vals.astype(F32))


def _gdn(proj3, small3, conv_w, a_log, dt_bias, norm_g):
    b, s, _ = proj3.shape
    c = min(STEP_TOKENS, s)
    g = c // CHUNK
    hc = N_HEADS * CHUNK
    qk = lambda j: pl.BlockSpec((1, c, QK_WIDTH), lambda bi, n: (bi, n, j))
    vv = lambda j: pl.BlockSpec((1, c, V_WIDTH), lambda bi, n: (bi, n, j))
    cw = lambda wd, j: pl.BlockSpec((4, wd), lambda bi, n: (0, j))
    row = lambda wd: pl.BlockSpec((1, wd), lambda bi, n: (0, 0))
    conv_w = conv_w.astype(F32)
    return pl.pallas_call(
        functools.partial(_gdn_kernel, n_chunks=g),
        out_shape=jax.ShapeDtypeStruct((b, s, V_WIDTH), BF16),
        grid=(b, s // c),
        in_specs=[qk(COL_GQ // QK_WIDTH), qk(COL_GK // QK_WIDTH), vv(COL_GV // V_WIDTH), vv(COL_GZ // V_WIDTH),
                  pl.BlockSpec((1, c, LANES), lambda bi, n: (bi, n, 0)),
                  cw(QK_WIDTH, 0), cw(QK_WIDTH, 1), cw(V_WIDTH, 1),
                  row(LANES), row(LANES), row(HEAD_V)],
        out_specs=pl.BlockSpec((1, c, V_WIDTH), lambda bi, n: (bi, n, 0)),
        scratch_shapes=[pltpu.VMEM((N_HEADS, HEAD_K, HEAD_V), F32),
                        pltpu.VMEM((SUBLANES, QK_WIDTH), F32), pltpu.VMEM((SUBLANES, QK_WIDTH), F32),
                        pltpu.VMEM((SUBLANES, V_WIDTH), F32),
                        pltpu.VMEM((g, hc, HEAD_K), BF16), pltpu.VMEM((g, hc, HEAD_V), F32),
                        pltpu.VMEM((g, N_HEADS * HEAD_K, HEAD_K), BF16), pltpu.VMEM((g, N_HEADS * HEAD_K, HEAD_V), F32)],
        compiler_params=_params(("arbitrary", "arbitrary")),
        name="gated_delta",
    )(proj3, proj3, proj3, proj3, small3, conv_w, conv_w, conv_w,
      _lane_row(a_log, N_HEADS), _lane_row(dt_bias, N_HEADS), norm_g.astype(F32)[None, :])


def _hgrn2_kernel(q_ref, f_ref, i_ref, g_ref, lbp_ref, gn_ref, o_ref, state_ref, *, n_chunks, layer):
    @pl.when(pl.program_id(1) == 0)
    def _():
        state_ref[...] = jnp.zeros_like(state_ref)

    c = CHUNK
    p = lbp_ref[...]
    e = jnp.exp(p - jnp.max(p, axis=0, keepdims=True))
    sm = e / jnp.sum(e, axis=0, keepdims=True)
    lb = jnp.zeros((1, QK_WIDTH), F32)
    for i in range(1, layer + 1):
        lb = lb + sm[i:i + 1]
    hf = f_ref[0].astype(F32)
    log_f = jnp.logaddexp(jax.nn.log_sigmoid(hf), jnp.log(lb) + jax.nn.log_sigmoid(-hf))
    k_all = (1.0 - lb) * jax.nn.sigmoid(-hf)
    q_all = q_ref[0].astype(F32) * (HEAD_K ** -0.5)
    gcum = _cumsum_rows(log_f, c)
    sub = HGRN_SUB
    nsub = c // sub
    sub_shift = sub.bit_length() - 1
    lane = lax.broadcasted_iota(jnp.int32, (sub, c), 1)
    lane_t = lane & (sub - 1)
    lane_blk = lane >> sub_shift
    srow = lax.broadcasted_iota(jnp.int32, (sub, 1), 0)
    zeros = lambda n: jnp.zeros((n, HEAD_K), F32)
    for j in range(n_chunks):
        rows = slice(j * c, (j + 1) * c)
        for h in range(N_HEADS):
            g = gcum[rows, h * HEAD_K:(h + 1) * HEAD_K]
            q = q_all[rows, h * HEAD_K:(h + 1) * HEAD_K]
            k = k_all[rows, h * HEAD_K:(h + 1) * HEAD_K]
            v = i_ref[0, rows, h * HEAD_V:(h + 1) * HEAD_V]
            q_cols, k_cols = [], []
            for i in range(1, nsub):
                r0 = i * sub
                ref = g[r0 - 1:r0]
                qt = q[r0:r0 + sub] * jnp.exp(g[r0:r0 + sub] - ref)
                q_cols.append(jnp.concatenate([zeros(r0), qt] + ([zeros(c - r0 - sub)] if r0 + sub < c else []), axis=0))
                k_cols.append(jnp.concatenate([k[:r0] * jnp.exp(ref - g[:r0]), zeros(c - r0)], axis=0))
            off_t = _dot_nt(jnp.concatenate(k_cols, axis=1).astype(BF16),
                            jnp.concatenate(q_cols, axis=1).astype(BF16))
            blocks = []
            for i in range(nsub):
                r0 = i * sub
                gb = g[r0:r0 + sub]
                qb = q[r0:r0 + sub]
                kb = k[r0:r0 + sub]
                acc_t = jnp.zeros((sub, c), F32)
                for t in range(sub):
                    w = kb * jnp.exp(gb[t:t + 1] - gb) * qb[t:t + 1]
                    col = jnp.where(srow <= t, jnp.sum(w, axis=1, keepdims=True), 0.0)
                    acc_t = jnp.where(lane_t == t, col, acc_t)
                blocks.append(jnp.where(lane_blk == i, acc_t, off_t[r0:r0 + sub]))
            attn_t = jnp.concatenate(blocks, axis=0)
            st = state_ref[h]
            out = _dot_tn(attn_t.astype(BF16), v) + _dot_nt((q * jnp.exp(g)).astype(BF16), st.astype(BF16))
            g_last = g[c - 1:c]
            state_ref[h] = st * jnp.exp(g_last) + _dot_tn(v, (k * jnp.exp(g_last - g)).astype(BF16))
            on = out * lax.rsqrt(jnp.mean(out * out, axis=-1, keepdims=True) + NORM_EPS) * gn_ref[...]
            gate = g_ref[0, rows, h * HEAD_V:(h + 1) * HEAD_V].astype(F32)
            o_ref[0, rows, h * HEAD_V:(h + 1) * HEAD_V] = (jax.nn.sigmoid(gate) * on).astype(o_ref.dtype)


def _hgrn2(proj3, lb_param, norm_g, layer):
    b, s, _ = proj3.shape
    c = min(STEP_TOKENS, s)
    qk = lambda j: pl.BlockSpec((1, c, QK_WIDTH), lambda bi, n: (bi, n, j))
    vv = lambda j: pl.BlockSpec((1, c, V_WIDTH), lambda bi, n: (bi, n, j))
    return pl.pallas_call(
        functools.partial(_hgrn2_kernel, n_chunks=c // CHUNK, layer=layer),
        out_shape=jax.ShapeDtypeStruct((b, s, V_WIDTH), BF16),
        grid=(b, s // c),
        in_specs=[qk(COL_HQ // QK_WIDTH), qk(COL_HF // QK_WIDTH), vv(COL_HI // V_WIDTH), vv(COL_HG // V_WIDTH),
                  pl.BlockSpec((DEPTH, QK_WIDTH), lambda bi, n: (0, 0)),
                  pl.BlockSpec((1, HEAD_V), lambda bi, n: (0, 0))],
        out_specs=pl.BlockSpec((1, c, V_WIDTH), lambda bi, n: (bi, n, 0)),
        scratch_shapes=[pltpu.VMEM((N_HEADS, HEAD_V, HEAD_K), F32)],
        compiler_params=_params(("arbitrary", "arbitrary")),
        name="hgrn2",
    )(proj3, proj3, proj3, proj3, lb_param.astype(F32), norm_g.astype(F32)[None, :])


def _merge_kernel(yr_ref, yg_ref, yh_ref, m0_ref, m1_ref, m2_ref, mb_ref, wb_ref, wo_ref, x_ref, lg_ref, lb_ref,
                  xo_ref, pa_ref, pb_ref):
    merged = None
    for n, (y_ref, m_ref) in enumerate(((yr_ref, m0_ref), (yg_ref, m1_ref), (yh_ref, m2_ref))):
        gate = jax.nn.sigmoid(m_ref[...].astype(F32) + mb_ref[:, n * D_MODEL:(n + 1) * D_MODEL])
        term = gate * _dot(y_ref[...], wb_ref[n])
        merged = term if merged is None else merged + term
    h = _dot(merged.astype(BF16), wo_ref[...])
    xn = _layer_norm(ALPHA * x_ref[...] + h, lg_ref[...], lb_ref[...])
    xo_ref[...] = xn
    pa_ref[...], pb_ref[...] = _pack_row(xn)


def _merge(y_ret, y_gdn, y_hg, proj, merge_b, wb, wo, xf, ln_g, ln_b, tm=512):
    t = xf.shape[0]
    tm = min(tm, t)
    tile = lambda j: pl.BlockSpec((tm, D_MODEL), lambda i: (i, j))
    row = lambda wd: pl.BlockSpec((1, wd), lambda i: (0, 0))
    packed = pl.BlockSpec((tm, D_MODEL // 4), lambda i: (i, 0))
    packed_shape = jax.ShapeDtypeStruct((t, D_MODEL // 4), jnp.int32)
    return pl.pallas_call(
        _merge_kernel,
        out_shape=(jax.ShapeDtypeStruct((t, D_MODEL), F32), packed_shape, packed_shape),
        grid=(t // tm,),
        in_specs=[tile(0), tile(0), tile(0),
                  tile(COL_MG // D_MODEL), tile(COL_MG // D_MODEL + 1), tile(COL_MG // D_MODEL + 2),
                  row(N_BRANCH * D_MODEL),
                  pl.BlockSpec((N_BRANCH, V_WIDTH, D_MODEL), lambda i: (0, 0, 0)),
                  pl.BlockSpec((D_MODEL, D_MODEL), lambda i: (0, 0)),
                  tile(0), row(D_MODEL), row(D_MODEL)],
        out_specs=(tile(0), packed, packed),
        compiler_params=_params(("arbitrary",)),
        name="merge_out_ln",
    )(y_ret, y_gdn, y_hg, proj, proj, proj, merge_b.astype(F32)[None, :], wb, wo, xf,
      ln_g.astype(F32)[None, :], ln_b.astype(F32)[None, :])


def _router_kernel(x_ref, rw_ref, rb_ref, ct_ref, rk_ref, cnt_ref, sel_ref, carry_ref, before_ref):
    @pl.when(pl.program_id(0) == 0)
    def _():
        carry_ref[...] = jnp.zeros_like(carry_ref)
        tm = before_ref.shape[0]
        earlier = lax.broadcasted_iota(jnp.int32, (tm, tm), 0) < lax.broadcasted_iota(jnp.int32, (tm, tm), 1)
        before_ref[...] = jnp.where(earlier, 1.0, 0.0).astype(BF16)

    x = x_ref[...]
    w = rw_ref[...]
    xh = x.astype(BF16)
    xl = (x - xh.astype(F32)).astype(BF16)
    wh = w.astype(BF16)
    wl = (w - wh.astype(F32)).astype(BF16)
    logits = (_dot(xh, wh) + (_dot(xl, wh) + _dot(xh, wl))).T[:N_EXPERTS]
    scores = jax.nn.sigmoid(logits)
    biased = scores + rb_ref[...]
    epg = EXPERTS_PER_GROUP
    brow = [biased[e:e + 1] for e in range(N_EXPERTS)]
    srow = [scores[e:e + 1] for e in range(N_EXPERTS)]
    gscore = []
    for g in range(N_GROUPS):
        a, b, c, d = brow[g * epg:(g + 1) * epg]
        hi1, lo1, hi2, lo2 = jnp.maximum(a, b), jnp.minimum(a, b), jnp.maximum(c, d), jnp.minimum(c, d)
        gscore.append(jnp.maximum(hi1, hi2) + jnp.maximum(jnp.minimum(hi1, hi2), jnp.maximum(lo1, lo2)))
    best = jnp.zeros(gscore[0].shape, jnp.int32)
    top = gscore[0]
    for g in range(1, N_GROUPS):
        upd = gscore[g] > top
        best = jnp.where(upd, g, best)
        top = jnp.where(upd, gscore[g], top)

    def pick(rows, j):
        out = rows[(N_GROUPS - 1) * epg + j]
        for g in range(N_GROUPS - 2, -1, -1):
            out = jnp.where(best == g, rows[g * epg + j], out)
        return out

    bv = [pick(brow, j) for j in range(epg)]
    sv = [pick(srow, j) for j in range(epg)]
    chosen = []
    for j in range(epg):
        rank = jnp.zeros(best.shape, jnp.int32)
        for i in range(epg):
            if i == j:
                continue
            ahead = (bv[i] > bv[j]) if i > j else (bv[i] >= bv[j])
            rank = rank + ahead.astype(jnp.int32)
        chosen.append(rank < 2)
    den = None
    for j in range(epg):
        term = jnp.where(chosen[j], sv[j], 0.0)
        den = term if den is None else den + term
    for g in range(N_GROUPS):
        for j in range(epg):
            e = g * epg + j
            hit = jnp.logical_and(chosen[j], best == g)
            ct_ref[e:e + 1, :] = jnp.where(hit, sv[j] / den, 0.0)
            sel_ref[e:e + 1, :] = jnp.where(hit, 1.0, 0.0)
    sel = sel_ref[...]
    carry = carry_ref[:, 0:1]
    rank = _dot(sel.astype(BF16), before_ref[...]) + carry
    rk_ref[...] = jnp.where(sel > 0.0, rank, -1.0).astype(jnp.int32)
    total = jnp.broadcast_to(carry + jnp.sum(sel, axis=1, keepdims=True), carry_ref.shape)
    carry_ref[...] = total
    cnt_ref[...] = total


def _router(xf, router_w, router_b, tm=1024):
    t = xf.shape[0]
    tm = min(tm, t)
    col = pl.BlockSpec((N_EXPERTS, tm), lambda i: (0, i))
    return pl.pallas_call(
        _router_kernel,
        out_shape=(jax.ShapeDtypeStruct((N_EXPERTS, t), F32), jax.ShapeDtypeStruct((N_EXPERTS, t), jnp.int32),
                   jax.ShapeDtypeStruct((N_EXPERTS, LANES), F32)),
        grid=(t // tm,),
        in_specs=[pl.BlockSpec((tm, D_MODEL), lambda i: (i, 0)),
                  pl.BlockSpec((D_MODEL, LANES), lambda i: (0, 0)),
                  pl.BlockSpec((N_EXPERTS, 1), lambda i: (0, 0))],
        out_specs=(col, col, pl.BlockSpec((N_EXPERTS, LANES), lambda i: (0, 0))),
        scratch_shapes=[pltpu.VMEM((N_EXPERTS, tm), F32), pltpu.VMEM((N_EXPERTS, LANES), F32),
                        pltpu.VMEM((tm, tm), BF16)],
        compiler_params=_params(("arbitrary",)),
        name="router",
    )(xf, jnp.pad(router_w.astype(F32), ((0, 0), (0, LANES - N_EXPERTS))), router_b.astype(F32)[:, None])


def _sc_mesh():
    return plsc.VectorSubcoreMesh(core_axis_name="core", subcore_axis_name="subcore",
                                  num_cores=SC_CORES, num_subcores=SC_SUBCORES)


def _sc_scatter_rows(src, idx0, idx1, n_out):
    n, w = src.shape

    @pl.kernel(out_type=jax.ShapeDtypeStruct((n_out, w), src.dtype), mesh=_sc_mesh(), scratch_types=[],
               name="dispatch_rows")
    def run(x_hbm, i0_hbm, i1_hbm, o_hbm):
        def body(x_vmem, i0_vmem, i1_vmem):
            pltpu.sync_copy(x_vmem, o_hbm.at[i0_vmem.at[0]])
            pltpu.sync_copy(x_vmem, o_hbm.at[i1_vmem.at[0]])

        pltpu.emit_pipeline(
            body, grid=(n // SC_WINDOW,),
            in_specs=[pl.BlockSpec((SC_WINDOW, w), index_map=lambda i: (i, 0)),
                      pl.BlockSpec((1, SC_WINDOW), index_map=lambda i: (0, i)),
                      pl.BlockSpec((1, SC_WINDOW), index_map=lambda i: (0, i))],
            out_specs=[], core_axis_name=("core", "subcore"), dimension_semantics=(pltpu.PARALLEL,),
        )(x_hbm, i0_hbm, i1_hbm)

    return run(src, idx0.reshape(1, n), idx1.reshape(1, n))


def _sc_gather_rows(src, idx):
    n = idx.shape[0]
    w = src.shape[1]

    @pl.kernel(out_type=jax.ShapeDtypeStruct((n, w), src.dtype), mesh=_sc_mesh(), scratch_types=[],
               name="collect_rows")
    def run(x_hbm, i_hbm, o_hbm):
        def body(i_vmem, o_vmem):
            pltpu.sync_copy(x_hbm.at[i_vmem.at[0]], o_vmem)

        pltpu.emit_pipeline(
            body, grid=(n // SC_WINDOW,),
            in_specs=[pl.BlockSpec((1, SC_WINDOW), index_map=lambda i: (0, i))],
            out_specs=[pl.BlockSpec((SC_WINDOW, w), index_map=lambda i: (i, 0))],
            core_axis_name=("core", "subcore"), dimension_semantics=(pltpu.PARALLEL,),
        )(i_hbm, o_hbm)

    return run(src, idx.reshape(1, n))


def _dispatch_plan(rank_t, comb_t, counts):
    tg = EXPERT_TILE
    t = rank_t.shape[1]
    n_tiles = 2 * t // tg + N_EXPERTS
    cnt = counts[:, 0].astype(jnp.int32)
    tiles = (cnt + tg - 1) // tg
    tile_end = jnp.cumsum(tiles)
    seg_start = (tile_end - tiles) * tg
    n_active = tile_end[-1]
    tile_ids = jnp.arange(n_tiles, dtype=jnp.int32)
    tile_expert = jnp.sum(tile_end[None, :] <= jnp.minimum(tile_ids, n_active - 1)[:, None], axis=1).astype(jnp.int32)
    routed = rank_t >= 0
    pos = seg_start[:, None] + rank_t
    order = jnp.cumsum(routed.astype(jnp.int32), axis=0)
    first = jnp.logical_and(routed, order == 1)
    second = jnp.logical_and(routed, order == 2)
    dest0 = jnp.sum(jnp.where(first, pos, 0), axis=0)
    dest1 = jnp.sum(jnp.where(second, pos, 0), axis=0)
    w01 = jnp.stack([jnp.sum(jnp.where(first, comb_t, 0.0), axis=0), jnp.sum(jnp.where(second, comb_t, 0.0), axis=0)], axis=1)
    return dest0, dest1, w01, tile_expert, n_active.reshape(1).astype(jnp.int32), n_tiles * tg


def _expert_kernel(te_ref, na_ref, xa_ref, xb_ref, wg_ref, wu_ref, wd_ref, ya_ref, yb_ref):
    del te_ref
    live = pl.program_id(0) < na_ref[0]

    @pl.when(live)
    def _():
        x = _unpack_row(xa_ref[...], xb_ref[...]).astype(BF16)
        h = jax.nn.silu(_dot(x, wg_ref[0].astype(BF16))) * _dot(x, wu_ref[0].astype(BF16))
        ya_ref[...], yb_ref[...] = _pack_row(_dot(h.astype(BF16), wd_ref[0].astype(BF16)))

    @pl.when(jnp.logical_not(live))
    def _():
        ya_ref[...] = jnp.zeros_like(ya_ref)
        yb_ref[...] = jnp.zeros_like(yb_ref)


def _experts(xsa, xsb, wg, wu, wd, tile_expert, n_active):
    n, q = xsa.shape
    tg = EXPERT_TILE
    rows = pl.BlockSpec((tg, q), lambda i, te, na: (i, 0))
    grid_spec = pltpu.PrefetchScalarGridSpec(
        num_scalar_prefetch=2, grid=(n // tg,),
        in_specs=[rows, rows,
                  pl.BlockSpec((1, D_MODEL, D_EXPERT), lambda i, te, na: (te[i], 0, 0)),
                  pl.BlockSpec((1, D_MODEL, D_EXPERT), lambda i, te, na: (te[i], 0, 0)),
                  pl.BlockSpec((1, D_EXPERT, D_MODEL), lambda i, te, na: (te[i], 0, 0))],
        out_specs=(rows, rows))
    shape = jax.ShapeDtypeStruct((n, q), jnp.int32)
    return pl.pallas_call(
        _expert_kernel, grid_spec=grid_spec, out_shape=(shape, shape),
        compiler_params=_params(("arbitrary",)), name="experts",
    )(tile_expert, n_active, xsa, xsb, wg, wu, wd)


def _combine_kernel(g0a_ref, g0b_ref, g1a_ref, g1b_ref, w_ref, x_ref, lg_ref, lb_ref, xo_ref, xbo_ref):
    y0 = _unpack_row(g0a_ref[...], g0b_ref[...])
    y1 = _unpack_row(g1a_ref[...], g1b_ref[...])
    w = w_ref[...]
    xn = _layer_norm(ALPHA * x_ref[...] + (w[:, 0:1] * y0 + w[:, 1:2] * y1), lg_ref[...], lb_ref[...])
    xo_ref[...] = xn
    xbo_ref[...] = xn.astype(BF16)


def _combine(g0a, g0b, g1a, g1b, w01, xf, ln_g, ln_b, tm=1024):
    t = xf.shape[0]
    tm = min(tm, t)
    tile = pl.BlockSpec((tm, D_MODEL), lambda i: (i, 0))
    packed = pl.BlockSpec((tm, D_MODEL // 4), lambda i: (i, 0))
    row = pl.BlockSpec((1, D_MODEL), lambda i: (0, 0))
    return pl.pallas_call(
        _combine_kernel,
        out_shape=(jax.ShapeDtypeStruct((t, D_MODEL), F32), jax.ShapeDtypeStruct((t, D_MODEL), BF16)),
        grid=(t // tm,),
        in_specs=[packed, packed, packed, packed, pl.BlockSpec((tm, 2), lambda i: (i, 0)), tile, row, row],
        out_specs=(tile, tile),
        compiler_params=_params(("arbitrary",)),
        name="combine_ln",
    )(g0a, g0b, g1a, g1b, w01, xf, ln_g.astype(F32)[None, :], ln_b.astype(F32)[None, :])


def kernel(x, positions, w_in, gdn_conv_w, gdn_a_log, gdn_dt_bias, gdn_norm_g, hgrn_lb, hgrn_norm_g, merge_b,
           w_branch, w_out, ln1_g, ln1_b, router_w, router_b, moe_w_gate, moe_w_up, moe_w_down, ln2_g, ln2_b):
    b, s, d = x.shape
    t = b * s
    assert d == D_MODEL and w_in.shape[-1] == N_MAIN + SMALL_WIDTH

    w_main = jnp.concatenate([w_in[:, :, :SMALL_START].astype(BF16), w_in[:, :, SMALL_START + SMALL_WIDTH:].astype(BF16)], axis=-1)
    w_small = jnp.pad(w_in[:, :, SMALL_START:SMALL_START + SMALL_WIDTH], ((0, 0), (0, 0), (0, LANES - SMALL_WIDTH))).astype(BF16)
    wb = w_branch.astype(BF16)
    wo = w_out.astype(BF16)
    wg, wu, wd = moe_w_gate, moe_w_up, moe_w_down

    cos, sin = _rope_tables(positions)
    cos3 = cos.reshape(b, s, HEAD_K)
    sin3 = sin.reshape(b, s, HEAD_K)

    xf = x.reshape(t, d).astype(F32)
    xb = xf.astype(BF16)
    for l in range(DEPTH):
        proj = _in_proj(xb, w_main[l], BF16, tm=2048, tn=1536)
        small = _in_proj(xb, w_small[l], F32, tm=2048, tn=LANES)
        proj3 = proj.reshape(b, s, N_MAIN)
        y_ret = _retention(proj3, cos3, sin3)
        y_gdn = _gdn(proj3, small.reshape(b, s, LANES), gdn_conv_w[l], gdn_a_log[l], gdn_dt_bias[l], gdn_norm_g[l])
        y_hg = _hgrn2(proj3, hgrn_lb, hgrn_norm_g[l], l)
        xf, pa, pb = _merge(y_ret.reshape(t, V_WIDTH), y_gdn.reshape(t, V_WIDTH), y_hg.reshape(t, V_WIDTH), proj,
                            merge_b[l], wb[l], wo[l], xf, ln1_g[l], ln1_b[l])
        comb_t, rank_t, counts = _router(xf, router_w, router_b)
        dest0, dest1, w01, tile_expert, n_active, n_rows = _dispatch_plan(rank_t, comb_t, counts)
        ysa, ysb = _experts(_sc_scatter_rows(pa, dest0, dest1, n_rows), _sc_scatter_rows(pb, dest0, dest1, n_rows),
                            wg[l], wu[l], wd[l], tile_expert, n_active)
        xf, xb = _combine(_sc_gather_rows(ysa, dest0), _sc_gather_rows(ysb, dest0),
                          _sc_gather_rows(ysa, dest1), _sc_gather_rows(ysb, dest1), w01, xf, ln2_g[l], ln2_b[l])
    return xf.reshape(b, s, d).astype(x.dtype)
```

```python
import functools
import math

import jax
import jax.numpy as jnp
from jax import lax
from jax.experimental import pallas as pl
from jax.experimental.pallas import tpu as pltpu
from jax.experimental.pallas import tpu_sc as plsc

F32 = jnp.float32
BF16 = jnp.bfloat16

D_MODEL = 1024
DEPTH = 4
N_HEADS = 4
HEAD_V = 256
HEAD_K = 128
QK_WIDTH = N_HEADS * HEAD_K
V_WIDTH = N_HEADS * HEAD_V
N_BRANCH = 3
ROPE_BASE = 10000.0
RET_DECAY_OFFSET = 5.0
N_EXPERTS = 16
N_GROUPS = 4
EXPERTS_PER_GROUP = N_EXPERTS // N_GROUPS
D_EXPERT = D_MODEL // 2
ALPHA = (2 * DEPTH) ** 0.25
LN_EPS = 1e-5
NORM_EPS = 1e-6

SMALL_START = 2 * QK_WIDTH + 2 * V_WIDTH + (2 * QK_WIDTH + V_WIDTH) + V_WIDTH
SMALL_WIDTH = 2 * N_HEADS
HALF_WIDTH = 6144
COL_RQ, COL_RK, COL_RV, COL_RG = 0, 512, 1024, 2048
COL_GQ, COL_GK, COL_GV, COL_GZ = 3072, 3584, 4096, 5120
COL_HQ, COL_HF, COL_HI, COL_HG = 0, 512, 1024, 2048
COL_MG = 3072

LANES = 128
SUBLANES = 8
VMEM_LIMIT = 56 * 1024 * 1024

RET_CHUNK = 256
CHUNK = 64
STEP_TOKENS = 256
HGRN_SUB = 16
EXPERT_TILE = 512
SC_CORES = 2
SC_SUBCORES = 16
SC_WINDOW = 128


def _dot(a, b):
    return jnp.dot(a, b, preferred_element_type=F32)


def _dot_nt(a, b):
    return lax.dot_general(a, b, (((1,), (1,)), ((), ())), preferred_element_type=F32)


def _dot_tn(a, b):
    return lax.dot_general(a, b, (((0,), (0,)), ((), ())), preferred_element_type=F32)


def _cumsum_rows(x, seg):
    assert seg & (seg - 1) == 0
    pos = lax.broadcasted_iota(jnp.int32, x.shape, 0) & (seg - 1)
    s = 1
    while s < seg:
        x = x + jnp.where(pos >= s, pltpu.roll(x, s, axis=0), 0.0)
        s *= 2
    return x


def _pack_pair(a, b):
    ua = lax.bitcast_convert_type(a.astype(BF16).astype(F32), jnp.int32)
    ub = lax.bitcast_convert_type(b.astype(BF16).astype(F32), jnp.int32)
    return lax.shift_right_logical(ua, 16) | (ub & jnp.int32(-65536))


def _unpack_pair(w):
    a = lax.bitcast_convert_type(lax.shift_left(w, 16), F32)
    b = lax.bitcast_convert_type(w & jnp.int32(-65536), F32)
    return a, b


def _pack_row(x):
    q = D_MODEL // 4
    return _pack_pair(x[:, :q], x[:, q:2 * q]), _pack_pair(x[:, 2 * q:3 * q], x[:, 3 * q:])


def _unpack_row(pa, pb):
    return jnp.concatenate(_unpack_pair(pa) + _unpack_pair(pb), axis=1)


def _layer_norm(z, g, b):
    mu = jnp.mean(z, axis=-1, keepdims=True)
    zc = z - mu
    var = jnp.mean(zc * zc, axis=-1, keepdims=True)
    return zc * lax.rsqrt(var + LN_EPS) * g + b


def _params(sem):
    return pltpu.CompilerParams(dimension_semantics=sem, vmem_limit_bytes=VMEM_LIMIT)


def _rope_kernel(pos_ref, invf_ref, cos_ref, sin_ref):
    ang = pos_ref[...].astype(F32) * invf_ref[...]
    lane = lax.broadcasted_iota(jnp.int32, ang.shape, 1)
    cos_ref[...] = jnp.cos(ang)
    s = jnp.sin(ang)
    sin_ref[...] = jnp.where(lane < HEAD_K // 2, -s, s)


def _rope_tables(positions):
    t = positions.size
    half = HEAD_K // 2
    inv_freq = 1.0 / (ROPE_BASE ** jnp.linspace(0.0, 1.0, half, dtype=F32))
    invf = jnp.concatenate([inv_freq, inv_freq])[None, :]
    ts = min(t, 2048)
    return pl.pallas_call(
        _rope_kernel,
        out_shape=(jax.ShapeDtypeStruct((t, HEAD_K), F32), jax.ShapeDtypeStruct((t, HEAD_K), F32)),
        grid=(t // ts,),
        in_specs=[pl.BlockSpec((ts, 1), lambda i: (i, 0)), pl.BlockSpec((1, HEAD_K), lambda i: (0, 0))],
        out_specs=(pl.BlockSpec((ts, HEAD_K), lambda i: (i, 0)), pl.BlockSpec((ts, HEAD_K), lambda i: (i, 0))),
        compiler_params=_params(("arbitrary",)),
        name="rope_tables",
    )(positions.reshape(t, 1), invf)


def _matmul_kernel(x_ref, w_ref, o_ref):
    o_ref[...] = _dot(x_ref[...], w_ref[0]).astype(o_ref.dtype)


def _matmul_small_kernel(x_ref, w_ref, ws_ref, o_ref, s_ref):
    x = x_ref[...]
    o_ref[...] = _dot(x, w_ref[0]).astype(o_ref.dtype)

    @pl.when(pl.program_id(1) == 0)
    def _():
        s_ref[...] = _dot(x, ws_ref[0])


def _in_proj(xb, w, layer, n, w_small=None, tm=2048, tn=1536):
    t, d = xb.shape
    tm = min(tm, t)
    x_spec = pl.BlockSpec((tm, d), lambda i, j: (i, 0))
    w_spec = pl.BlockSpec((1, d, tn), lambda i, j: (layer, 0, j))
    o_spec = pl.BlockSpec((tm, tn), lambda i, j: (i, j))
    o_shape = jax.ShapeDtypeStruct((t, n), BF16)
    if w_small is None:
        kern, in_specs, out_specs, out_shape, args = _matmul_kernel, [x_spec, w_spec], o_spec, o_shape, (xb, w)
    else:
        kern = _matmul_small_kernel
        in_specs = [x_spec, w_spec, pl.BlockSpec((1, d, LANES), lambda i, j: (layer, 0, 0))]
        out_specs = (o_spec, pl.BlockSpec((tm, LANES), lambda i, j: (i, 0)))
        out_shape = (o_shape, jax.ShapeDtypeStruct((t, LANES), F32))
        args = (xb, w, w_small)
    return pl.pallas_call(
        kern, out_shape=out_shape, grid=(t // tm, n // tn), in_specs=in_specs, out_specs=out_specs,
        compiler_params=_params(("arbitrary", "arbitrary")), name="in_proj",
    )(*args)


def _retention_kernel(q_ref, k_ref, v_ref, g_ref, cos_ref, sin_ref, o_ref, state_ref, intra_ref, qdec_ref, kdec_ref,
                      *, chunk):
    @pl.when(pl.program_id(1) == 0)
    def _():
        state_ref[...] = jnp.zeros_like(state_ref)

    c = chunk
    log_gamma = [math.log1p(-(2.0 ** (-RET_DECAY_OFFSET - h))) for h in range(N_HEADS)]

    @pl.when(jnp.logical_and(pl.program_id(0) == 0, pl.program_id(1) == 0))
    def _():
        t_col = lax.broadcasted_iota(jnp.int32, (c, HEAD_K), 0).astype(F32)
        rel = (lax.broadcasted_iota(jnp.int32, (c, c), 0) - lax.broadcasted_iota(jnp.int32, (c, c), 1)).astype(F32)
        for h, lg in enumerate(log_gamma):
            intra_ref[h] = jnp.where(rel >= 0, jnp.exp(lg * rel), 0.0) * (HEAD_K ** -0.5)
            qdec_ref[h] = jnp.exp(lg * (t_col + 1.0))
            kdec_ref[h] = jnp.exp(lg * (c - 1.0 - t_col)) * (HEAD_K ** -0.5)

    cos = cos_ref[0]
    sin = sin_ref[0]
    for h, lg in enumerate(log_gamma):
        q = q_ref[0, :, h * HEAD_K:(h + 1) * HEAD_K].astype(F32)
        k = k_ref[0, :, h * HEAD_K:(h + 1) * HEAD_K].astype(F32)
        q = q * cos + pltpu.roll(q, HEAD_K // 2, axis=1) * sin
        k = k * cos + pltpu.roll(k, HEAD_K // 2, axis=1) * sin
        v = v_ref[0, :, h * HEAD_V:(h + 1) * HEAD_V]
        scores = _dot_nt(q.astype(BF16), k.astype(BF16)) * intra_ref[h]
        st = state_ref[h]
        out = _dot(scores.astype(BF16), v) + _dot((q * qdec_ref[h]).astype(BF16), st.astype(BF16))
        state_ref[h] = math.exp(lg * c) * st + _dot_tn((k * kdec_ref[h]).astype(BF16), v)
        mu = jnp.mean(out, axis=-1, keepdims=True)
        oc = out - mu
        var = jnp.mean(oc * oc, axis=-1, keepdims=True)
        gate = g_ref[0, :, h * HEAD_V:(h + 1) * HEAD_V].astype(F32)
        o_ref[0, :, h * HEAD_V:(h + 1) * HEAD_V] = (jax.nn.silu(gate) * oc * lax.rsqrt(var + NORM_EPS)).astype(o_ref.dtype)


def _retention(proj3, cos3, sin3):
    b, s, _ = proj3.shape
    c = min(RET_CHUNK, s)
    qk = lambda j: pl.BlockSpec((1, c, QK_WIDTH), lambda bi, n: (bi, n, j))
    vv = lambda j: pl.BlockSpec((1, c, V_WIDTH), lambda bi, n: (bi, n, j))
    tab = pl.BlockSpec((1, c, HEAD_K), lambda bi, n: (bi, n, 0))
    return pl.pallas_call(
        functools.partial(_retention_kernel, chunk=c),
        out_shape=jax.ShapeDtypeStruct((b, s, V_WIDTH), BF16),
        grid=(b, s // c),
        in_specs=[qk(COL_RQ // QK_WIDTH), qk(COL_RK // QK_WIDTH), vv(COL_RV // V_WIDTH), vv(COL_RG // V_WIDTH), tab, tab],
        out_specs=pl.BlockSpec((1, c, V_WIDTH), lambda bi, n: (bi, n, 0)),
        scratch_shapes=[pltpu.VMEM((N_HEADS, HEAD_K, HEAD_V), F32), pltpu.VMEM((N_HEADS, c, c), F32),
                        pltpu.VMEM((N_HEADS, c, HEAD_K), F32), pltpu.VMEM((N_HEADS, c, HEAD_K), F32)],
        compiler_params=_params(("arbitrary", "arbitrary")),
        name="retention",
    )(proj3, proj3, proj3, proj3, cos3, sin3)


def _conv_silu(x, tail_ref, w):
    c = x.shape[0]
    prev = tail_ref[...]
    row8 = lax.broadcasted_iota(jnp.int32, (SUBLANES, 1), 0)
    y = w[3:4] * x
    for j in (1, 2, 3):
        xs = pltpu.roll(x, j, axis=0)
        head = jnp.where(row8 >= j, xs[:SUBLANES], pltpu.roll(prev, j, axis=0))
        xs = jnp.concatenate([head, xs[SUBLANES:]], axis=0)
        y = y + w[3 - j:4 - j] * xs
    tail_ref[...] = x[c - SUBLANES:]
    return y * jax.nn.sigmoid(y)


def _unit_lower_inverses(mats, order):
    n = mats[0].shape[0]
    eye = (lax.broadcasted_iota(jnp.int32, (n, n), 0) == lax.broadcasted_iota(jnp.int32, (n, n), 1)).astype(F32)
    ps = [-a for a in mats]
    invs = [eye + p for p in ps]
    k = 1
    while 2 * k < order:
        pbs = [p.astype(BF16) for p in ps]
        ps = [_dot(pb, pb) for pb in pbs]
        invs = [inv + _dot(inv.astype(BF16), p.astype(BF16)) for inv, p in zip(invs, ps)]
        k *= 2
    return invs


def _stack_heads(x, rows, width):
    return jnp.concatenate([x[rows, h * width:(h + 1) * width] for h in range(N_HEADS)], axis=0)


def _gdn_kernel(qp_ref, kp_ref, vp_ref, z_ref, sm_ref, cwq_ref, cwk_ref, cwv_ref, alog_ref, dtb_ref, gn_ref,
                o_ref, state_ref, tq_ref, tk_ref, tv_ref, qs_ref, os_ref, ms_ref, ns_ref, *, n_chunks):
    @pl.when(pl.program_id(1) == 0)
    def _():
        state_ref[...] = jnp.zeros_like(state_ref)
        tq_ref[...] = jnp.zeros_like(tq_ref)
        tk_ref[...] = jnp.zeros_like(tk_ref)
        tv_ref[...] = jnp.zeros_like(tv_ref)

    c = CHUNK
    hc = N_HEADS * c
    q_all = _conv_silu(qp_ref[0].astype(F32), tq_ref, cwq_ref[...])
    k_all = _conv_silu(kp_ref[0].astype(F32), tk_ref, cwk_ref[...])
    v_all = _conv_silu(vp_ref[0].astype(F32), tv_ref, cwv_ref[...])
    sm = sm_ref[0]
    beta_all = jax.nn.sigmoid(sm)
    g_all = -jnp.exp(alog_ref[...]) * jax.nn.softplus(sm + dtb_ref[...])
    gc_all = _cumsum_rows(g_all, c)
    gc_t = gc_all.T
    ri = lax.broadcasted_iota(jnp.int32, (hc, hc), 0)
    ci = lax.broadcasted_iota(jnp.int32, (hc, hc), 1)
    shift = c.bit_length() - 1
    lower = jnp.logical_and(ri >> shift == ci >> shift, ri >= ci)
    diag = ri == ci
    row_head = lax.broadcasted_iota(jnp.int32, (hc, 1), 0) >> shift
    col = lambda x, rows, j: jnp.concatenate([x[rows, j + h:j + h + 1] for h in range(N_HEADS)], axis=0)

    pre = []
    for j in range(n_chunks):
        rows = slice(j * c, (j + 1) * c)
        kraw = _stack_heads(k_all, rows, HEAD_K)
        qraw = _stack_heads(q_all, rows, HEAD_K)
        kn = kraw * lax.rsqrt(jnp.sum(kraw * kraw, axis=-1, keepdims=True) + NORM_EPS)
        qn = qraw * lax.rsqrt(jnp.sum(qraw * qraw, axis=-1, keepdims=True) + NORM_EPS) * (HEAD_K ** -0.5)
        beta = col(beta_all, rows, 0)
        gc = col(gc_all, rows, N_HEADS)
        gr = jnp.concatenate([gc_t[N_HEADS + h:N_HEADS + h + 1, rows] for h in range(N_HEADS)], axis=1)
        last = slice((j + 1) * c - 1, (j + 1) * c)
        g_last = [gc_all[last, N_HEADS + h:N_HEADS + h + 1] for h in range(N_HEADS)]
        gl = jnp.concatenate([jnp.broadcast_to(g, (c, 1)) for g in g_last], axis=0)
        pre.append(dict(kn=kn, qn=qn, kb=kn * beta, vb=_stack_heads(v_all, rows, HEAD_V) * beta, gc=gc, gr=gr,
                        eg=jnp.exp(gc), gl=gl, egl=[jnp.exp(g) for g in g_last]))
    decays = [jnp.where(lower, jnp.exp(p["gc"] - p["gr"]), 0.0) for p in pre]
    kqs = [_dot_nt(jnp.concatenate([p["kb"], p["qn"]], axis=0).astype(BF16), p["kn"].astype(BF16)) for p in pre]
    tms = _unit_lower_inverses([jnp.where(diag, 0.0, kq[:hc] * d) for kq, d in zip(kqs, decays)], c)
    uws = [_dot(tm.astype(BF16), jnp.concatenate([p["vb"], p["kb"] * p["eg"]], axis=1).astype(BF16)).astype(BF16)
           for tm, p in zip(tms, pre)]
    aos = [_dot((kq[hc:] * d).astype(BF16), uw) for kq, d, uw in zip(kqs, decays, uws)]
    for j, (p, uw, ao) in enumerate(zip(pre, uws, aos)):
        kd = p["kn"] * jnp.exp(p["gl"] - p["gc"])
        kd_wide = jnp.concatenate([jnp.where(row_head == h, kd, 0.0) for h in range(N_HEADS)], axis=1)
        nm = _dot_tn(kd_wide.astype(BF16), uw)
        qs_ref[j] = (p["qn"] * p["eg"] - ao[:, HEAD_V:]).astype(BF16)
        os_ref[j] = ao[:, :HEAD_V]
        ns_ref[j] = nm[:, :HEAD_V]
        ms_ref[j] = nm[:, HEAD_V:].astype(BF16)

    for j in range(n_chunks):
        sts = [state_ref[h] for h in range(N_HEADS)]
        rrs = [_dot(jnp.concatenate([qs_ref[j, h * c:(h + 1) * c], ms_ref[j, h * HEAD_K:(h + 1) * HEAD_K]], axis=0),
                    sts[h].astype(BF16)) for h in range(N_HEADS)]
        for h in range(N_HEADS):
            state_ref[h] = pre[j]["egl"][h] * sts[h] - rrs[h][c:] + ns_ref[j, h * HEAD_K:(h + 1) * HEAD_K]
        for h in range(N_HEADS):
            out = rrs[h][:c] + os_ref[j, h * c:(h + 1) * c]
            on = out * lax.rsqrt(jnp.mean(out * out, axis=-1, keepdims=True) + NORM_EPS) * gn_ref[...]
            z = z_ref[0, j * c:(j + 1) * c, h * HEAD_V:(h + 1) * HEAD_V].astype(F32)
            o_ref[0, j * c:(j + 1) * c, h * HEAD_V:(h + 1) * HEAD_V] = (jax.nn.silu(z) * on).astype(o_ref.dtype)


def _lane_row(vals, start):
    return jnp.zeros((1, LANES), F32).at[0, start:start + vals.shape[0]].set(vals.astype(F32))


def _gdn(proj3, small3, conv_w, a_log, dt_bias, norm_g):
    b, s, _ = proj3.shape
    c = min(STEP_TOKENS, s)
    g = c // CHUNK
    hc = N_HEADS * CHUNK
    qk = lambda j: pl.BlockSpec((1, c, QK_WIDTH), lambda bi, n: (bi, n, j))
    vv = lambda j: pl.BlockSpec((1, c, V_WIDTH), lambda bi, n: (bi, n, j))
    cw = lambda wd, j: pl.BlockSpec((4, wd), lambda bi, n: (0, j))
    row = lambda wd: pl.BlockSpec((1, wd), lambda bi, n: (0, 0))
    conv_w = conv_w.astype(F32)
    return pl.pallas_call(
        functools.partial(_gdn_kernel, n_chunks=g),
        out_shape=jax.ShapeDtypeStruct((b, s, V_WIDTH), BF16),
        grid=(b, s // c),
        in_specs=[qk(COL_GQ // QK_WIDTH), qk(COL_GK // QK_WIDTH), vv(COL_GV // V_WIDTH), vv(COL_GZ // V_WIDTH),
                  pl.BlockSpec((1, c, LANES), lambda bi, n: (bi, n, 0)),
                  cw(QK_WIDTH, 0), cw(QK_WIDTH, 1), cw(V_WIDTH, 1),
                  row(LANES), row(LANES), row(HEAD_V)],
        out_specs=pl.BlockSpec((1, c, V_WIDTH), lambda bi, n: (bi, n, 0)),
        scratch_shapes=[pltpu.VMEM((N_HEADS, HEAD_K, HEAD_V), F32),
                        pltpu.VMEM((SUBLANES, QK_WIDTH), F32), pltpu.VMEM((SUBLANES, QK_WIDTH), F32),
                        pltpu.VMEM((SUBLANES, V_WIDTH), F32),
                        pltpu.VMEM((g, hc, HEAD_K), BF16), pltpu.VMEM((g, hc, HEAD_V), F32),
                        pltpu.VMEM((g, N_HEADS * HEAD_K, HEAD_K), BF16), pltpu.VMEM((g, N_HEADS * HEAD_K, HEAD_V), F32)],
        compiler_params=_params(("arbitrary", "arbitrary")),
        name="gated_delta",
    )(proj3, proj3, proj3, proj3, small3, conv_w, conv_w, conv_w,
      _lane_row(a_log, N_HEADS), _lane_row(dt_bias, N_HEADS), norm_g.astype(F32)[None, :])


def _hgrn2_kernel(q_ref, f_ref, i_ref, g_ref, lbp_ref, gn_ref, o_ref, state_ref, *, n_chunks, layer):
    @pl.when(pl.program_id(1) == 0)
    def _():
        state_ref[...] = jnp.zeros_like(state_ref)

    c = CHUNK
    p = lbp_ref[...]
    e = jnp.exp(p - jnp.max(p, axis=0, keepdims=True))
    sm = e / jnp.sum(e, axis=0, keepdims=True)
    lb = jnp.zeros((1, QK_WIDTH), F32)
    for i in range(1, layer + 1):
        lb = lb + sm[i:i + 1]
    hf = f_ref[0].astype(F32)
    log_f = jnp.logaddexp(jax.nn.log_sigmoid(hf), jnp.log(lb) + jax.nn.log_sigmoid(-hf))
    k_all = (1.0 - lb) * jax.nn.sigmoid(-hf)
    q_all = q_ref[0].astype(F32) * (HEAD_K ** -0.5)
    gcum = _cumsum_rows(log_f, c)
    sub = HGRN_SUB
    nsub = c // sub
    sub_shift = sub.bit_length() - 1
    lane = lax.broadcasted_iota(jnp.int32, (sub, c), 1)
    lane_t = lane & (sub - 1)
    lane_blk = lane >> sub_shift
    srow = lax.broadcasted_iota(jnp.int32, (sub, 1), 0)
    zeros = lambda n: jnp.zeros((n, HEAD_K), F32)
    for j in range(n_chunks):
        rows = slice(j * c, (j + 1) * c)
        for h in range(N_HEADS):
            g = gcum[rows, h * HEAD_K:(h + 1) * HEAD_K]
            q = q_all[rows, h * HEAD_K:(h + 1) * HEAD_K]
            k = k_all[rows, h * HEAD_K:(h + 1) * HEAD_K]
            v = i_ref[0, rows, h * HEAD_V:(h + 1) * HEAD_V]
            q_cols, k_cols = [], []
            for i in range(1, nsub):
                r0 = i * sub
                ref = g[r0 - 1:r0]
                qt = q[r0:r0 + sub] * jnp.exp(g[r0:r0 + sub] - ref)
                q_cols.append(jnp.concatenate([zeros(r0), qt] + ([zeros(c - r0 - sub)] if r0 + sub < c else []), axis=0))
                k_cols.append(jnp.concatenate([k[:r0] * jnp.exp(ref - g[:r0]), zeros(c - r0)], axis=0))
            off_t = _dot_nt(jnp.concatenate(k_cols, axis=1).astype(BF16),
                            jnp.concatenate(q_cols, axis=1).astype(BF16))
            blocks = []
            for i in range(nsub):
                r0 = i * sub
                gb = g[r0:r0 + sub]
                qb = q[r0:r0 + sub]
                kb = k[r0:r0 + sub]
                acc_t = jnp.zeros((sub, c), F32)
                for t in range(sub):
                    w = kb * jnp.exp(gb[t:t + 1] - gb) * qb[t:t + 1]
                    col = jnp.where(srow <= t, jnp.sum(w, axis=1, keepdims=True), 0.0)
                    acc_t = jnp.where(lane_t == t, col, acc_t)
                blocks.append(jnp.where(lane_blk == i, acc_t, off_t[r0:r0 + sub]))
            attn_t = jnp.concatenate(blocks, axis=0)
            st = state_ref[h]
            out = _dot_tn(attn_t.astype(BF16), v) + _dot_nt((q * jnp.exp(g)).astype(BF16), st.astype(BF16))
            g_last = g[c - 1:c]
            state_ref[h] = st * jnp.exp(g_last) + _dot_tn(v, (k * jnp.exp(g_last - g)).astype(BF16))
            on = out * lax.rsqrt(jnp.mean(out * out, axis=-1, keepdims=True) + NORM_EPS) * gn_ref[...]
            gate = g_ref[0, rows, h * HEAD_V:(h + 1) * HEAD_V].astype(F32)
            o_ref[0, rows, h * HEAD_V:(h + 1) * HEAD_V] = (jax.nn.sigmoid(gate) * on).astype(o_ref.dtype)


def _hgrn2(proj3, lb_param, norm_g, layer):
    b, s, _ = proj3.shape
    c = min(STEP_TOKENS, s)
    qk = lambda j: pl.BlockSpec((1, c, QK_WIDTH), lambda bi, n: (bi, n, j))
    vv = lambda j: pl.BlockSpec((1, c, V_WIDTH), lambda bi, n: (bi, n, j))
    return pl.pallas_call(
        functools.partial(_hgrn2_kernel, n_chunks=c // CHUNK, layer=layer),
        out_shape=jax.ShapeDtypeStruct((b, s, V_WIDTH), BF16),
        grid=(b, s // c),
        in_specs=[qk(COL_HQ // QK_WIDTH), qk(COL_HF // QK_WIDTH), vv(COL_HI // V_WIDTH), vv(COL_HG // V_WIDTH),
                  pl.BlockSpec((DEPTH, QK_WIDTH), lambda bi, n: (0, 0)),
                  pl.BlockSpec((1, HEAD_V), lambda bi, n: (0, 0))],
        out_specs=pl.BlockSpec((1, c, V_WIDTH), lambda bi, n: (bi, n, 0)),
        scratch_shapes=[pltpu.VMEM((N_HEADS, HEAD_V, HEAD_K), F32)],
        compiler_params=_params(("arbitrary", "arbitrary")),
        name="hgrn2",
    )(proj3, proj3, proj3, proj3, lb_param.astype(F32), norm_g.astype(F32)[None, :])


def _merge_kernel(yr_ref, yg_ref, yh_ref, m0_ref, m1_ref, m2_ref, mb_ref, wb_ref, wo_ref, x_ref, lg_ref, lb_ref,
                  xo_ref, pa_ref, pb_ref):
    merged = None
    for n, (y_ref, m_ref) in enumerate(((yr_ref, m0_ref), (yg_ref, m1_ref), (yh_ref, m2_ref))):
        gate = jax.nn.sigmoid(m_ref[...].astype(F32) + mb_ref[:, n * D_MODEL:(n + 1) * D_MODEL])
        term = gate * _dot(y_ref[...], wb_ref[0, n])
        merged = term if merged is None else merged + term
    h = _dot(merged.astype(BF16), wo_ref[0])
    xn = _layer_norm(ALPHA * x_ref[...] + h, lg_ref[...], lb_ref[...])
    xo_ref[...] = xn
    pa_ref[...], pb_ref[...] = _pack_row(xn)


def _merge(y_ret, y_gdn, y_hg, proj, merge_b, wb, wo, layer, xf, ln_g, ln_b, tm=512):
    t = xf.shape[0]
    tm = min(tm, t)
    tile = lambda j: pl.BlockSpec((tm, D_MODEL), lambda i: (i, j))
    row = lambda wd: pl.BlockSpec((1, wd), lambda i: (0, 0))
    packed = pl.BlockSpec((tm, D_MODEL // 4), lambda i: (i, 0))
    packed_shape = jax.ShapeDtypeStruct((t, D_MODEL // 4), jnp.int32)
    return pl.pallas_call(
        _merge_kernel,
        out_shape=(jax.ShapeDtypeStruct((t, D_MODEL), F32), packed_shape, packed_shape),
        grid=(t // tm,),
        in_specs=[tile(0), tile(0), tile(0),
                  tile(COL_MG // D_MODEL), tile(COL_MG // D_MODEL + 1), tile(COL_MG // D_MODEL + 2),
                  row(N_BRANCH * D_MODEL),
                  pl.BlockSpec((1, N_BRANCH, V_WIDTH, D_MODEL), lambda i: (layer, 0, 0, 0)),
                  pl.BlockSpec((1, D_MODEL, D_MODEL), lambda i: (layer, 0, 0)),
                  tile(0), row(D_MODEL), row(D_MODEL)],
        out_specs=(tile(0), packed, packed),
        compiler_params=_params(("arbitrary",)),
        name="merge_out_ln",
    )(y_ret, y_gdn, y_hg, proj, proj, proj, merge_b.astype(F32)[None, :], wb, wo, xf,
      ln_g.astype(F32)[None, :], ln_b.astype(F32)[None, :])


def _router_kernel(x_ref, rw_ref, rb_ref, ct_ref, rk_ref, cnt_ref, sel_ref, carry_ref):
    @pl.when(pl.program_id(0) == 0)
    def _():
        carry_ref[...] = jnp.zeros_like(carry_ref)

    x = x_ref[...]
    w = rw_ref[...]
    xh = x.astype(BF16)
    xl = (x - xh.astype(F32)).astype(BF16)
    wh = w.astype(BF16)
    wl = (w - wh.astype(F32)).astype(BF16)
    logits = (_dot(xh, wh) + (_dot(xl, wh) + _dot(xh, wl))).T[:N_EXPERTS]
    scores = jax.nn.sigmoid(logits)
    biased = scores + rb_ref[...]
    epg = EXPERTS_PER_GROUP
    brow = [biased[e:e + 1] for e in range(N_EXPERTS)]
    srow = [scores[e:e + 1] for e in range(N_EXPERTS)]
    gscore = []
    for g in range(N_GROUPS):
        a, b, c, d = brow[g * epg:(g + 1) * epg]
        hi1, lo1, hi2, lo2 = jnp.maximum(a, b), jnp.minimum(a, b), jnp.maximum(c, d), jnp.minimum(c, d)
        gscore.append(jnp.maximum(hi1, hi2) + jnp.maximum(jnp.minimum(hi1, hi2), jnp.maximum(lo1, lo2)))
    best = jnp.zeros(gscore[0].shape, jnp.int32)
    top = gscore[0]
    for g in range(1, N_GROUPS):
        upd = gscore[g] > top
        best = jnp.where(upd, g, best)
        top = jnp.where(upd, gscore[g], top)

    def pick(rows, j):
        out = rows[(N_GROUPS - 1) * epg + j]
        for g in range(N_GROUPS - 2, -1, -1):
            out = jnp.where(best == g, rows[g * epg + j], out)
        return out

    bv = [pick(brow, j) for j in range(epg)]
    sv = [pick(srow, j) for j in range(epg)]
    chosen = []
    for j in range(epg):
        rank = jnp.zeros(best.shape, jnp.int32)
        for i in range(epg):
            if i == j:
                continue
            ahead = (bv[i] > bv[j]) if i > j else (bv[i] >= bv[j])
            rank = rank + ahead.astype(jnp.int32)
        chosen.append(rank < 2)
    den = None
    for j in range(epg):
        term = jnp.where(chosen[j], sv[j], 0.0)
        den = term if den is None else den + term
    for g in range(N_GROUPS):
        for j in range(epg):
            e = g * epg + j
            hit = jnp.logical_and(chosen[j], best == g)
            ct_ref[e:e + 1, :] = jnp.where(hit, sv[j] / den, 0.0)
            sel_ref[e:e + 1, :] = jnp.where(hit, 1.0, 0.0)
    sel = sel_ref[...]
    tm = sel.shape[1]
    before = lax.broadcasted_iota(jnp.int32, (tm, tm), 0) < lax.broadcasted_iota(jnp.int32, (tm, tm), 1)
    carry = carry_ref[:, 0:1]
    rank = _dot(sel.astype(BF16), jnp.where(before, 1.0, 0.0).astype(BF16)) + carry
    rk_ref[...] = jnp.where(sel > 0.0, rank, -1.0).astype(jnp.int32)
    total = jnp.broadcast_to(carry + jnp.sum(sel, axis=1, keepdims=True), carry_ref.shape)
    carry_ref[...] = total
    cnt_ref[...] = total


def _router(xf, router_w, router_b, tm=1024):
    t = xf.shape[0]
    tm = min(tm, t)
    col = pl.BlockSpec((N_EXPERTS, tm), lambda i: (0, i))
    return pl.pallas_call(
        _router_kernel,
        out_shape=(jax.ShapeDtypeStruct((N_EXPERTS, t), F32), jax.ShapeDtypeStruct((N_EXPERTS, t), jnp.int32),
                   jax.ShapeDtypeStruct((N_EXPERTS, LANES), F32)),
        grid=(t // tm,),
        in_specs=[pl.BlockSpec((tm, D_MODEL), lambda i: (i, 0)),
                  pl.BlockSpec((D_MODEL, LANES), lambda i: (0, 0)),
                  pl.BlockSpec((N_EXPERTS, 1), lambda i: (0, 0))],
        out_specs=(col, col, pl.BlockSpec((N_EXPERTS, LANES), lambda i: (0, 0))),
        scratch_shapes=[pltpu.VMEM((N_EXPERTS, tm), F32), pltpu.VMEM((N_EXPERTS, LANES), F32)],
        compiler_params=_params(("arbitrary",)),
        name="router",
    )(xf, jnp.pad(router_w.astype(F32), ((0, 0), (0, LANES - N_EXPERTS))), router_b.astype(F32)[:, None])


def _sc_mesh():
    return plsc.VectorSubcoreMesh(core_axis_name="core", subcore_axis_name="subcore",
                                  num_cores=SC_CORES, num_subcores=SC_SUBCORES)


def _sc_scatter_rows(src, idx0, idx1, n_out):
    n, w = src.shape

    @pl.kernel(out_type=jax.ShapeDtypeStruct((n_out, w), src.dtype), mesh=_sc_mesh(), scratch_types=[],
               name="dispatch_rows")
    def run(x_hbm, i0_hbm, i1_hbm, o_hbm):
        def body(x_vmem, i0_vmem, i1_vmem):
            pltpu.sync_copy(x_vmem, o_hbm.at[i0_vmem.at[0]])
            pltpu.sync_copy(x_vmem, o_hbm.at[i1_vmem.at[0]])

        pltpu.emit_pipeline(
            body, grid=(n // SC_WINDOW,),
            in_specs=[pl.BlockSpec((SC_WINDOW, w), index_map=lambda i: (i, 0)),
                      pl.BlockSpec((1, SC_WINDOW), index_map=lambda i: (0, i)),
                      pl.BlockSpec((1, SC_WINDOW), index_map=lambda i: (0, i))],
            out_specs=[], core_axis_name=("core", "subcore"), dimension_semantics=(pltpu.PARALLEL,),
        )(x_hbm, i0_hbm, i1_hbm)

    return run(src, idx0.reshape(1, n), idx1.reshape(1, n))


def _sc_gather_rows(src, idx):
    n = idx.shape[0]
    w = src.shape[1]

    @pl.kernel(out_type=jax.ShapeDtypeStruct((n, w), src.dtype), mesh=_sc_mesh(), scratch_types=[],
               name="collect_rows")
    def run(x_hbm, i_hbm, o_hbm):
        def body(i_vmem, o_vmem):
            pltpu.sync_copy(x_hbm.at[i_vmem.at[0]], o_vmem)

        pltpu.emit_pipeline(
            body, grid=(n // SC_WINDOW,),
            in_specs=[pl.BlockSpec((1, SC_WINDOW), index_map=lambda i: (0, i))],
            out_specs=[pl.BlockSpec((SC_WINDOW, w), index_map=lambda i: (i, 0))],
            core_axis_name=("core", "subcore"), dimension_semantics=(pltpu.PARALLEL,),
        )(i_hbm, o_hbm)

    return run(src, idx.reshape(1, n))


def _dispatch_plan(rank_t, comb_t, counts):
    tg = EXPERT_TILE
    t = rank_t.shape[1]
    n_tiles = 2 * t // tg + N_EXPERTS
    cnt = counts[:, 0].astype(jnp.int32)
    tiles = (cnt + tg - 1) // tg
    tile_end = jnp.cumsum(tiles)
    seg_start = (tile_end - tiles) * tg
    n_active = tile_end[-1]
    tile_ids = jnp.arange(n_tiles, dtype=jnp.int32)
    tile_expert = jnp.sum(tile_end[None, :] <= jnp.minimum(tile_ids, n_active - 1)[:, None], axis=1).astype(jnp.int32)
    routed = rank_t >= 0
    pos = seg_start[:, None] + rank_t
    order = jnp.cumsum(routed.astype(jnp.int32), axis=0)
    first = jnp.logical_and(routed, order == 1)
    second = jnp.logical_and(routed, order == 2)
    dest0 = jnp.sum(jnp.where(first, pos, 0), axis=0)
    dest1 = jnp.sum(jnp.where(second, pos, 0), axis=0)
    w01 = jnp.stack([jnp.sum(jnp.where(first, comb_t, 0.0), axis=0), jnp.sum(jnp.where(second, comb_t, 0.0), axis=0)], axis=1)
    return dest0, dest1, w01, tile_expert, n_active.reshape(1).astype(jnp.int32), n_tiles * tg


def _expert_kernel(te_ref, na_ref, xa_ref, xb_ref, wg_ref, wu_ref, wd_ref, ya_ref, yb_ref):
    del te_ref
    live = pl.program_id(0) < na_ref[0]

    @pl.when(live)
    def _():
        x = _unpack_row(xa_ref[...], xb_ref[...]).astype(BF16)
        h = jax.nn.silu(_dot(x, wg_ref[0, 0].astype(BF16))) * _dot(x, wu_ref[0, 0].astype(BF16))
        ya_ref[...], yb_ref[...] = _pack_row(_dot(h.astype(BF16), wd_ref[0, 0].astype(BF16)))

    @pl.when(jnp.logical_not(live))
    def _():
        ya_ref[...] = jnp.zeros_like(ya_ref)
        yb_ref[...] = jnp.zeros_like(yb_ref)


def _experts(xsa, xsb, wg, wu, wd, layer, tile_expert, n_active):
    n, q = xsa.shape
    tg = EXPERT_TILE
    rows = pl.BlockSpec((tg, q), lambda i, te, na: (i, 0))
    grid_spec = pltpu.PrefetchScalarGridSpec(
        num_scalar_prefetch=2, grid=(n // tg,),
        in_specs=[rows, rows,
                  pl.BlockSpec((1, 1, D_MODEL, D_EXPERT), lambda i, te, na: (layer, te[i], 0, 0)),
                  pl.BlockSpec((1, 1, D_MODEL, D_EXPERT), lambda i, te, na: (layer, te[i], 0, 0)),
                  pl.BlockSpec((1, 1, D_EXPERT, D_MODEL), lambda i, te, na: (layer, te[i], 0, 0))],
        out_specs=(rows, rows))
    shape = jax.ShapeDtypeStruct((n, q), jnp.int32)
    return pl.pallas_call(
        _expert_kernel, grid_spec=grid_spec, out_shape=(shape, shape),
        compiler_params=_params(("arbitrary",)), name="experts",
    )(tile_expert, n_active, xsa, xsb, wg, wu, wd)


def _combine_kernel(g0a_ref, g0b_ref, g1a_ref, g1b_ref, w_ref, x_ref, lg_ref, lb_ref, xo_ref, xbo_ref):
    y0 = _unpack_row(g0a_ref[...], g0b_ref[...])
    y1 = _unpack_row(g1a_ref[...], g1b_ref[...])
    w = w_ref[...]
    xn = _layer_norm(ALPHA * x_ref[...] + (w[:, 0:1] * y0 + w[:, 1:2] * y1), lg_ref[...], lb_ref[...])
    xo_ref[...] = xn
    xbo_ref[...] = xn.astype(BF16)


def _combine(g0a, g0b, g1a, g1b, w01, xf, ln_g, ln_b, tm=1024):
    t = xf.shape[0]
    tm = min(tm, t)
    tile = pl.BlockSpec((tm, D_MODEL), lambda i: (i, 0))
    packed = pl.BlockSpec((tm, D_MODEL // 4), lambda i: (i, 0))
    row = pl.BlockSpec((1, D_MODEL), lambda i: (0, 0))
    return pl.pallas_call(
        _combine_kernel,
        out_shape=(jax.ShapeDtypeStruct((t, D_MODEL), F32), jax.ShapeDtypeStruct((t, D_MODEL), BF16)),
        grid=(t // tm,),
        in_specs=[packed, packed, packed, packed, pl.BlockSpec((tm, 2), lambda i: (i, 0)), tile, row, row],
        out_specs=(tile, tile),
        compiler_params=_params(("arbitrary",)),
        name="combine_ln",
    )(g0a, g0b, g1a, g1b, w01, xf, ln_g.astype(F32)[None, :], ln_b.astype(F32)[None, :])


def kernel(x, positions, w_in, gdn_conv_w, gdn_a_log, gdn_dt_bias, gdn_norm_g, hgrn_lb, hgrn_norm_g, merge_b,
           w_branch, w_out, ln1_g, ln1_b, router_w, router_b, moe_w_gate, moe_w_up, moe_w_down, ln2_g, ln2_b):
    b, s, d = x.shape
    t = b * s
    assert d == D_MODEL and w_in.shape[-1] == 2 * HALF_WIDTH + SMALL_WIDTH and SMALL_START == HALF_WIDTH

    w_bf = w_in.astype(BF16)
    w_hi = w_bf[:, :, SMALL_START + SMALL_WIDTH:]
    w_small = jnp.pad(w_bf[:, :, SMALL_START:SMALL_START + SMALL_WIDTH], ((0, 0), (0, 0), (0, LANES - SMALL_WIDTH)))
    wb = w_branch.astype(BF16)
    wo = w_out.astype(BF16)
    wg, wu, wd = moe_w_gate, moe_w_up, moe_w_down

    cos, sin = _rope_tables(positions)
    cos3 = cos.reshape(b, s, HEAD_K)
    sin3 = sin.reshape(b, s, HEAD_K)

    xf = x.reshape(t, d).astype(F32)
    xb = xf.astype(BF16)
    for l in range(DEPTH):
        proj_lo, small = _in_proj(xb, w_bf, l, HALF_WIDTH, w_small=w_small)
        proj_hi = _in_proj(xb, w_hi, l, HALF_WIDTH)
        lo3 = proj_lo.reshape(b, s, HALF_WIDTH)
        hi3 = proj_hi.reshape(b, s, HALF_WIDTH)
        y_ret = _retention(lo3, cos3, sin3)
        y_gdn = _gdn(lo3, small.reshape(b, s, LANES), gdn_conv_w[l], gdn_a_log[l], gdn_dt_bias[l], gdn_norm_g[l])
        y_hg = _hgrn2(hi3, hgrn_lb, hgrn_norm_g[l], l)
        xf, pa, pb = _merge(y_ret.reshape(t, V_WIDTH), y_gdn.reshape(t, V_WIDTH), y_hg.reshape(t, V_WIDTH), proj_hi,
                            merge_b[l], wb, wo, l, xf, ln1_g[l], ln1_b[l])
        comb_t, rank_t, counts = _router(xf, router_w, router_b)
        dest0, dest1, w01, tile_expert, n_active, n_rows = _dispatch_plan(rank_t, comb_t, counts)
        ysa, ysb = _experts(_sc_scatter_rows(pa, dest0, dest1, n_rows), _sc_scatter_rows(pb, dest0, dest1, n_rows),
                            wg, wu, wd, l, tile_expert, n_active)
        xf, xb = _combine(_sc_gather_rows(ysa, dest0), _sc_gather_rows(ysb, dest0),
                          _sc_gather_rows(ysa, dest1), _sc_gather_rows(ysb, dest1), w01, xf, ln2_g[l], ln2_b[l])
    return xf.reshape(b, s, d).astype(x.dtype)
```

```python
import functools
import math

import jax
import jax.numpy as jnp
from jax import lax
from jax.experimental import pallas as pl
from jax.experimental.pallas import tpu as pltpu
from jax.experimental.pallas import tpu_sc as plsc

F32 = jnp.float32
BF16 = jnp.bfloat16

D_MODEL = 1024
DEPTH = 4
N_HEADS = 4
HEAD_V = 256
HEAD_K = 128
QK_WIDTH = N_HEADS * HEAD_K
V_WIDTH = N_HEADS * HEAD_V
N_BRANCH = 3
ROPE_BASE = 10000.0
RET_DECAY_OFFSET = 5.0
N_EXPERTS = 16
N_GROUPS = 4
EXPERTS_PER_GROUP = N_EXPERTS // N_GROUPS
D_EXPERT = D_MODEL // 2
ALPHA = (2 * DEPTH) ** 0.25
LN_EPS = 1e-5
NORM_EPS = 1e-6

SMALL_START = 2 * QK_WIDTH + 2 * V_WIDTH + (2 * QK_WIDTH + V_WIDTH) + V_WIDTH
SMALL_WIDTH = 2 * N_HEADS
HALF_WIDTH = 6144
COL_RQ, COL_RK, COL_RV, COL_RG = 0, 512, 1024, 2048
COL_GQ, COL_GK, COL_GV, COL_GZ = 3072, 3584, 4096, 5120
COL_HQ, COL_HF, COL_HI, COL_HG = 0, 512, 1024, 2048
COL_MG = 3072

LANES = 128
SUBLANES = 8
VMEM_LIMIT = 56 * 1024 * 1024

RET_CHUNK = 256
CHUNK = 64
STEP_TOKENS = 512
HGRN_SUB = 16
EXPERT_TILE = 512
SC_CORES = 2
SC_SUBCORES = 16
SC_WINDOW = 128


def _dot(a, b):
    return jnp.dot(a, b, preferred_element_type=F32)


def _dot_nt(a, b):
    return lax.dot_general(a, b, (((1,), (1,)), ((), ())), preferred_element_type=F32)


def _dot_tn(a, b):
    return lax.dot_general(a, b, (((0,), (0,)), ((), ())), preferred_element_type=F32)


def _cumsum_rows(x, seg):
    assert seg & (seg - 1) == 0
    pos = lax.broadcasted_iota(jnp.int32, x.shape, 0) & (seg - 1)
    s = 1
    while s < seg:
        x = x + jnp.where(pos >= s, pltpu.roll(x, s, axis=0), 0.0)
        s *= 2
    return x


def _pack_pair(a, b):
    ua = lax.bitcast_convert_type(a.astype(BF16).astype(F32), jnp.int32)
    ub = lax.bitcast_convert_type(b.astype(BF16).astype(F32), jnp.int32)
    return lax.shift_right_logical(ua, 16) | (ub & jnp.int32(-65536))


def _unpack_pair(w):
    a = lax.bitcast_convert_type(lax.shift_left(w, 16), F32)
    b = lax.bitcast_convert_type(w & jnp.int32(-65536), F32)
    return a, b


def _pack_row(x):
    q = D_MODEL // 4
    return _pack_pair(x[:, :q], x[:, q:2 * q]), _pack_pair(x[:, 2 * q:3 * q], x[:, 3 * q:])


def _unpack_row(pa, pb):
    return jnp.concatenate(_unpack_pair(pa) + _unpack_pair(pb), axis=1)


def _layer_norm(z, g, b):
    mu = jnp.mean(z, axis=-1, keepdims=True)
    zc = z - mu
    var = jnp.mean(zc * zc, axis=-1, keepdims=True)
    return zc * lax.rsqrt(var + LN_EPS) * g + b


def _params(sem):
    return pltpu.CompilerParams(dimension_semantics=sem, vmem_limit_bytes=VMEM_LIMIT)


def _rope_kernel(pos_ref, invf_ref, cos_ref, sin_ref):
    ang = pos_ref[...].astype(F32) * invf_ref[...]
    lane = lax.broadcasted_iota(jnp.int32, ang.shape, 1)
    cos_ref[...] = jnp.cos(ang)
    s = jnp.sin(ang)
    sin_ref[...] = jnp.where(lane < HEAD_K // 2, -s, s)


def _rope_tables(positions):
    t = positions.size
    half = HEAD_K // 2
    inv_freq = 1.0 / (ROPE_BASE ** jnp.linspace(0.0, 1.0, half, dtype=F32))
    invf = jnp.concatenate([inv_freq, inv_freq])[None, :]
    ts = min(t, 2048)
    return pl.pallas_call(
        _rope_kernel,
        out_shape=(jax.ShapeDtypeStruct((t, HEAD_K), F32), jax.ShapeDtypeStruct((t, HEAD_K), F32)),
        grid=(t // ts,),
        in_specs=[pl.BlockSpec((ts, 1), lambda i: (i, 0)), pl.BlockSpec((1, HEAD_K), lambda i: (0, 0))],
        out_specs=(pl.BlockSpec((ts, HEAD_K), lambda i: (i, 0)), pl.BlockSpec((ts, HEAD_K), lambda i: (i, 0))),
        compiler_params=_params(("arbitrary",)),
        name="rope_tables",
    )(positions.reshape(t, 1), invf)


def _matmul_kernel(x_ref, w_ref, o_ref):
    o_ref[...] = _dot(x_ref[...], w_ref[0]).astype(o_ref.dtype)


def _matmul_small_kernel(x_ref, w_ref, ws_ref, o_ref, s_ref):
    x = x_ref[...]
    o_ref[...] = _dot(x, w_ref[0]).astype(o_ref.dtype)

    @pl.when(pl.program_id(1) == 0)
    def _():
        s_ref[...] = _dot(x, ws_ref[0])


def _in_proj(xb, w, layer, n, w_small=None, tm=2048, tn=1536):
    t, d = xb.shape
    tm = min(tm, t)
    x_spec = pl.BlockSpec((tm, d), lambda i, j: (i, 0))
    w_spec = pl.BlockSpec((1, d, tn), lambda i, j: (layer, 0, j))
    o_spec = pl.BlockSpec((tm, tn), lambda i, j: (i, j))
    o_shape = jax.ShapeDtypeStruct((t, n), BF16)
    if w_small is None:
        kern, in_specs, out_specs, out_shape, args = _matmul_kernel, [x_spec, w_spec], o_spec, o_shape, (xb, w)
    else:
        kern = _matmul_small_kernel
        in_specs = [x_spec, w_spec, pl.BlockSpec((1, d, LANES), lambda i, j: (layer, 0, 0))]
        out_specs = (o_spec, pl.BlockSpec((tm, LANES), lambda i, j: (i, 0)))
        out_shape = (o_shape, jax.ShapeDtypeStruct((t, LANES), F32))
        args = (xb, w, w_small)
    return pl.pallas_call(
        kern, out_shape=out_shape, grid=(t // tm, n // tn), in_specs=in_specs, out_specs=out_specs,
        compiler_params=_params(("arbitrary", "arbitrary")), name="in_proj",
    )(*args)


def _retention_kernel(q_ref, k_ref, v_ref, g_ref, cos_ref, sin_ref, o_ref, state_ref, intra_ref, qdec_ref, kdec_ref,
                      *, chunk):
    @pl.when(pl.program_id(1) == 0)
    def _():
        state_ref[...] = jnp.zeros_like(state_ref)

    c = chunk
    log_gamma = [math.log1p(-(2.0 ** (-RET_DECAY_OFFSET - h))) for h in range(N_HEADS)]

    @pl.when(jnp.logical_and(pl.program_id(0) == 0, pl.program_id(1) == 0))
    def _():
        t_col = lax.broadcasted_iota(jnp.int32, (c, HEAD_K), 0).astype(F32)
        rel = (lax.broadcasted_iota(jnp.int32, (c, c), 0) - lax.broadcasted_iota(jnp.int32, (c, c), 1)).astype(F32)
        for h, lg in enumerate(log_gamma):
            intra_ref[h] = jnp.where(rel >= 0, jnp.exp(lg * rel), 0.0) * (HEAD_K ** -0.5)
            qdec_ref[h] = jnp.exp(lg * (t_col + 1.0))
            kdec_ref[h] = jnp.exp(lg * (c - 1.0 - t_col)) * (HEAD_K ** -0.5)

    cos = cos_ref[0]
    sin = sin_ref[0]
    for h, lg in enumerate(log_gamma):
        q = q_ref[0, :, h * HEAD_K:(h + 1) * HEAD_K].astype(F32)
        k = k_ref[0, :, h * HEAD_K:(h + 1) * HEAD_K].astype(F32)
        q = q * cos + pltpu.roll(q, HEAD_K // 2, axis=1) * sin
        k = k * cos + pltpu.roll(k, HEAD_K // 2, axis=1) * sin
        v = v_ref[0, :, h * HEAD_V:(h + 1) * HEAD_V]
        scores = _dot_nt(q.astype(BF16), k.astype(BF16)) * intra_ref[h]
        st = state_ref[h]
        out = _dot(scores.astype(BF16), v) + _dot((q * qdec_ref[h]).astype(BF16), st.astype(BF16))
        state_ref[h] = math.exp(lg * c) * st + _dot_tn((k * kdec_ref[h]).astype(BF16), v)
        mu = jnp.mean(out, axis=-1, keepdims=True)
        oc = out - mu
        var = jnp.mean(oc * oc, axis=-1, keepdims=True)
        gate = g_ref[0, :, h * HEAD_V:(h + 1) * HEAD_V].astype(F32)
        o_ref[0, :, h * HEAD_V:(h + 1) * HEAD_V] = (jax.nn.silu(gate) * oc * lax.rsqrt(var + NORM_EPS)).astype(o_ref.dtype)


def _retention(proj3, cos3, sin3):
    b, s, _ = proj3.shape
    c = min(RET_CHUNK, s)
    qk = lambda j: pl.BlockSpec((1, c, QK_WIDTH), lambda bi, n: (bi, n, j))
    vv = lambda j: pl.BlockSpec((1, c, V_WIDTH), lambda bi, n: (bi, n, j))
    tab = pl.BlockSpec((1, c, HEAD_K), lambda bi, n: (bi, n, 0))
    return pl.pallas_call(
        functools.partial(_retention_kernel, chunk=c),
        out_shape=jax.ShapeDtypeStruct((b, s, V_WIDTH), BF16),
        grid=(b, s // c),
        in_specs=[qk(COL_RQ // QK_WIDTH), qk(COL_RK // QK_WIDTH), vv(COL_RV // V_WIDTH), vv(COL_RG // V_WIDTH), tab, tab],
        out_specs=pl.BlockSpec((1, c, V_WIDTH), lambda bi, n: (bi, n, 0)),
        scratch_shapes=[pltpu.VMEM((N_HEADS, HEAD_K, HEAD_V), F32), pltpu.VMEM((N_HEADS, c, c), F32),
                        pltpu.VMEM((N_HEADS, c, HEAD_K), F32), pltpu.VMEM((N_HEADS, c, HEAD_K), F32)],
        compiler_params=_params(("arbitrary", "arbitrary")),
        name="retention",
    )(proj3, proj3, proj3, proj3, cos3, sin3)


def _conv_silu(x, tail_ref, w):
    c = x.shape[0]
    prev = tail_ref[...]
    row8 = lax.broadcasted_iota(jnp.int32, (SUBLANES, 1), 0)
    y = w[3:4] * x
    for j in (1, 2, 3):
        xs = pltpu.roll(x, j, axis=0)
        head = jnp.where(row8 >= j, xs[:SUBLANES], pltpu.roll(prev, j, axis=0))
        xs = jnp.concatenate([head, xs[SUBLANES:]], axis=0)
        y = y + w[3 - j:4 - j] * xs
    tail_ref[...] = x[c - SUBLANES:]
    return y * jax.nn.sigmoid(y)


def _unit_lower_inverses(mats, order):
    n = mats[0].shape[0]
    eye = (lax.broadcasted_iota(jnp.int32, (n, n), 0) == lax.broadcasted_iota(jnp.int32, (n, n), 1)).astype(F32)
    ps = [-a for a in mats]
    invs = [eye + p for p in ps]
    k = 1
    while 2 * k < order:
        pbs = [p.astype(BF16) for p in ps]
        ps = [_dot(pb, pb) for pb in pbs]
        invs = [inv + _dot(inv.astype(BF16), p.astype(BF16)) for inv, p in zip(invs, ps)]
        k *= 2
    return invs


def _stack_heads(x, rows, width):
    return jnp.concatenate([x[rows, h * width:(h + 1) * width] for h in range(N_HEADS)], axis=0)


def _gdn_kernel(qp_ref, kp_ref, vp_ref, z_ref, sm_ref, cwq_ref, cwk_ref, cwv_ref, alog_ref, dtb_ref, gn_ref,
                o_ref, state_ref, tq_ref, tk_ref, tv_ref, qs_ref, os_ref, ms_ref, ns_ref, *, n_chunks):
    @pl.when(pl.program_id(1) == 0)
    def _():
        state_ref[...] = jnp.zeros_like(state_ref)
        tq_ref[...] = jnp.zeros_like(tq_ref)
        tk_ref[...] = jnp.zeros_like(tk_ref)
        tv_ref[...] = jnp.zeros_like(tv_ref)

    c = CHUNK
    hc = N_HEADS * c
    q_all = _conv_silu(qp_ref[0].astype(F32), tq_ref, cwq_ref[...])
    k_all = _conv_silu(kp_ref[0].astype(F32), tk_ref, cwk_ref[...])
    v_all = _conv_silu(vp_ref[0].astype(F32), tv_ref, cwv_ref[...])
    sm = sm_ref[0]
    beta_all = jax.nn.sigmoid(sm)
    g_all = -jnp.exp(alog_ref[...]) * jax.nn.softplus(sm + dtb_ref[...])
    gc_all = _cumsum_rows(g_all, c)
    gc_t = gc_all.T
    ri = lax.broadcasted_iota(jnp.int32, (hc, hc), 0)
    ci = lax.broadcasted_iota(jnp.int32, (hc, hc), 1)
    shift = c.bit_length() - 1
    lower = jnp.logical_and(ri >> shift == ci >> shift, ri >= ci)
    diag = ri == ci
    row_head = lax.broadcasted_iota(jnp.int32, (hc, 1), 0) >> shift
    col = lambda x, rows, j: jnp.concatenate([x[rows, j + h:j + h + 1] for h in range(N_HEADS)], axis=0)

    pre = []
    for j in range(n_chunks):
        rows = slice(j * c, (j + 1) * c)
        kraw = _stack_heads(k_all, rows, HEAD_K)
        qraw = _stack_heads(q_all, rows, HEAD_K)
        kn = kraw * lax.rsqrt(jnp.sum(kraw * kraw, axis=-1, keepdims=True) + NORM_EPS)
        qn = qraw * lax.rsqrt(jnp.sum(qraw * qraw, axis=-1, keepdims=True) + NORM_EPS) * (HEAD_K ** -0.5)
        beta = col(beta_all, rows, 0)
        gc = col(gc_all, rows, N_HEADS)
        gr = jnp.concatenate([gc_t[N_HEADS + h:N_HEADS + h + 1, rows] for h in range(N_HEADS)], axis=1)
        last = slice((j + 1) * c - 1, (j + 1) * c)
        g_last = [gc_all[last, N_HEADS + h:N_HEADS + h + 1] for h in range(N_HEADS)]
        gl = jnp.concatenate([jnp.broadcast_to(g, (c, 1)) for g in g_last], axis=0)
        pre.append(dict(kn=kn, qn=qn, kb=kn * beta, vb=_stack_heads(v_all, rows, HEAD_V) * beta, gc=gc, gr=gr,
                        eg=jnp.exp(gc), gl=gl, egl=[jnp.exp(g) for g in g_last]))
    decays = [jnp.where(lower, jnp.exp(p["gc"] - p["gr"]), 0.0) for p in pre]
    kqs = [_dot_nt(jnp.concatenate([p["kb"], p["qn"]], axis=0).astype(BF16), p["kn"].astype(BF16)) for p in pre]
    tms = _unit_lower_inverses([jnp.where(diag, 0.0, kq[:hc] * d) for kq, d in zip(kqs, decays)], c)
    uws = [_dot(tm.astype(BF16), jnp.concatenate([p["vb"], p["kb"] * p["eg"]], axis=1).astype(BF16)).astype(BF16)
           for tm, p in zip(tms, pre)]
    aos = [_dot((kq[hc:] * d).astype(BF16), uw) for kq, d, uw in zip(kqs, decays, uws)]
    for j, (p, uw, ao) in enumerate(zip(pre, uws, aos)):
        kd = p["kn"] * jnp.exp(p["gl"] - p["gc"])
        kd_wide = jnp.concatenate([jnp.where(row_head == h, kd, 0.0) for h in range(N_HEADS)], axis=1)
        nm = _dot_tn(kd_wide.astype(BF16), uw)
        qs_ref[j] = (p["qn"] * p["eg"] - ao[:, HEAD_V:]).astype(BF16)
        os_ref[j] = ao[:, :HEAD_V]
        ns_ref[j] = nm[:, :HEAD_V]
        ms_ref[j] = nm[:, HEAD_V:].astype(BF16)

    for j in range(n_chunks):
        sts = [state_ref[h] for h in range(N_HEADS)]
        rrs = [_dot(jnp.concatenate([qs_ref[j, h * c:(h + 1) * c], ms_ref[j, h * HEAD_K:(h + 1) * HEAD_K]], axis=0),
                    sts[h].astype(BF16)) for h in range(N_HEADS)]
        for h in range(N_HEADS):
            state_ref[h] = pre[j]["egl"][h] * sts[h] - rrs[h][c:] + ns_ref[j, h * HEAD_K:(h + 1) * HEAD_K]
        for h in range(N_HEADS):
            out = rrs[h][:c] + os_ref[j, h * c:(h + 1) * c]
            on = out * lax.rsqrt(jnp.mean(out * out, axis=-1, keepdims=True) + NORM_EPS) * gn_ref[...]
            z = z_ref[0, j * c:(j + 1) * c, h * HEAD_V:(h + 1) * HEAD_V].astype(F32)
            o_ref[0, j * c:(j + 1) * c, h * HEAD_V:(h + 1) * HEAD_V] = (jax.nn.silu(z) * on).astype(o_ref.dtype)


def _lane_row(vals, start):
    return jnp.zeros((1, LANES), F32).at[0, start:start + vals.shape[0]].set(vals.astype(F32))


def _gdn(proj3, small3, conv_w, a_log, dt_bias, norm_g):
    b, s, _ = proj3.shape
    c = min(STEP_TOKENS, s)
    g = c // CHUNK
    hc = N_HEADS * CHUNK
    qk = lambda j: pl.BlockSpec((1, c, QK_WIDTH), lambda bi, n: (bi, n, j))
    vv = lambda j: pl.BlockSpec((1, c, V_WIDTH), lambda bi, n: (bi, n, j))
    cw = lambda wd, j: pl.BlockSpec((4, wd), lambda bi, n: (0, j))
    row = lambda wd: pl.BlockSpec((1, wd), lambda bi, n: (0, 0))
    conv_w = conv_w.astype(F32)
    return pl.pallas_call(
        functools.partial(_gdn_kernel, n_chunks=g),
        out_shape=jax.ShapeDtypeStruct((b, s, V_WIDTH), BF16),
        grid=(b, s // c),
        in_specs=[qk(COL_GQ // QK_WIDTH), qk(COL_GK // QK_WIDTH), vv(COL_GV // V_WIDTH), vv(COL_GZ // V_WIDTH),
                  pl.BlockSpec((1, c, LANES), lambda bi, n: (bi, n, 0)),
                  cw(QK_WIDTH, 0), cw(QK_WIDTH, 1), cw(V_WIDTH, 1),
                  row(LANES), row(LANES), row(HEAD_V)],
        out_specs=pl.BlockSpec((1, c, V_WIDTH), lambda bi, n: (bi, n, 0)),
        scratch_shapes=[pltpu.VMEM((N_HEADS, HEAD_K, HEAD_V), F32),
                        pltpu.VMEM((SUBLANES, QK_WIDTH), F32), pltpu.VMEM((SUBLANES, QK_WIDTH), F32),
                        pltpu.VMEM((SUBLANES, V_WIDTH), F32),
                        pltpu.VMEM((g, hc, HEAD_K), BF16), pltpu.VMEM((g, hc, HEAD_V), F32),
                        pltpu.VMEM((g, N_HEADS * HEAD_K, HEAD_K), BF16), pltpu.VMEM((g, N_HEADS * HEAD_K, HEAD_V), F32)],
        compiler_params=_params(("arbitrary", "arbitrary")),
        name="gated_delta",
    )(proj3, proj3, proj3, proj3, small3, conv_w, conv_w, conv_w,
      _lane_row(a_log, N_HEADS), _lane_row(dt_bias, N_HEADS), norm_g.astype(F32)[None, :])


def _hgrn2_kernel(q_ref, f_ref, i_ref, g_ref, lbp_ref, gn_ref, o_ref, state_ref, *, n_chunks, layer):
    @pl.when(pl.program_id(1) == 0)
    def _():
        state_ref[...] = jnp.zeros_like(state_ref)

    c = CHUNK
    p = lbp_ref[...]
    e = jnp.exp(p - jnp.max(p, axis=0, keepdims=True))
    sm = e / jnp.sum(e, axis=0, keepdims=True)
    lb = jnp.zeros((1, QK_WIDTH), F32)
    for i in range(1, layer + 1):
        lb = lb + sm[i:i + 1]
    hf = f_ref[0].astype(F32)
    log_f = jnp.logaddexp(jax.nn.log_sigmoid(hf), jnp.log(lb) + jax.nn.log_sigmoid(-hf))
    k_all = (1.0 - lb) * jax.nn.sigmoid(-hf)
    q_all = q_ref[0].astype(F32) * (HEAD_K ** -0.5)
    gcum = _cumsum_rows(log_f, c)
    sub = HGRN_SUB
    nsub = c // sub
    sub_shift = sub.bit_length() - 1
    lane = lax.broadcasted_iota(jnp.int32, (sub, c), 1)
    lane_t = lane & (sub - 1)
    lane_blk = lane >> sub_shift
    srow = lax.broadcasted_iota(jnp.int32, (sub, 1), 0)
    zeros = lambda n: jnp.zeros((n, HEAD_K), F32)
    for j in range(n_chunks):
        rows = slice(j * c, (j + 1) * c)
        for h in range(N_HEADS):
            g = gcum[rows, h * HEAD_K:(h + 1) * HEAD_K]
            q = q_all[rows, h * HEAD_K:(h + 1) * HEAD_K]
            k = k_all[rows, h * HEAD_K:(h + 1) * HEAD_K]
            v = i_ref[0, rows, h * HEAD_V:(h + 1) * HEAD_V]
            q_cols, k_cols = [], []
            for i in range(1, nsub):
                r0 = i * sub
                ref = g[r0 - 1:r0]
                qt = q[r0:r0 + sub] * jnp.exp(g[r0:r0 + sub] - ref)
                q_cols.append(jnp.concatenate([zeros(r0), qt] + ([zeros(c - r0 - sub)] if r0 + sub < c else []), axis=0))
                k_cols.append(jnp.concatenate([k[:r0] * jnp.exp(ref - g[:r0]), zeros(c - r0)], axis=0))
            off_t = _dot_nt(jnp.concatenate(k_cols, axis=1).astype(BF16),
                            jnp.concatenate(q_cols, axis=1).astype(BF16))
            blocks = []
            for i in range(nsub):
                r0 = i * sub
                gb = g[r0:r0 + sub]
                qb = q[r0:r0 + sub]
                kb = k[r0:r0 + sub]
                acc_t = jnp.zeros((sub, c), F32)
                for t in range(sub):
                    w = kb * jnp.exp(gb[t:t + 1] - gb) * qb[t:t + 1]
                    col = jnp.where(srow <= t, jnp.sum(w, axis=1, keepdims=True), 0.0)
                    acc_t = jnp.where(lane_t == t, col, acc_t)
                blocks.append(jnp.where(lane_blk == i, acc_t, off_t[r0:r0 + sub]))
            attn_t = jnp.concatenate(blocks, axis=0)
            st = state_ref[h]
            out = _dot_tn(attn_t.astype(BF16), v) + _dot_nt((q * jnp.exp(g)).astype(BF16), st.astype(BF16))
            g_last = g[c - 1:c]
            state_ref[h] = st * jnp.exp(g_last) + _dot_tn(v, (k * jnp.exp(g_last - g)).astype(BF16))
            on = out * lax.rsqrt(jnp.mean(out * out, axis=-1, keepdims=True) + NORM_EPS) * gn_ref[...]
            gate = g_ref[0, rows, h * HEAD_V:(h + 1) * HEAD_V].astype(F32)
            o_ref[0, rows, h * HEAD_V:(h + 1) * HEAD_V] = (jax.nn.sigmoid(gate) * on).astype(o_ref.dtype)


def _hgrn2(proj3, lb_param, norm_g, layer):
    b, s, _ = proj3.shape
    c = min(STEP_TOKENS, s)
    qk = lambda j: pl.BlockSpec((1, c, QK_WIDTH), lambda bi, n: (bi, n, j))
    vv = lambda j: pl.BlockSpec((1, c, V_WIDTH), lambda bi, n: (bi, n, j))
    return pl.pallas_call(
        functools.partial(_hgrn2_kernel, n_chunks=c // CHUNK, layer=layer),
        out_shape=jax.ShapeDtypeStruct((b, s, V_WIDTH), BF16),
        grid=(b, s // c),
        in_specs=[qk(COL_HQ // QK_WIDTH), qk(COL_HF // QK_WIDTH), vv(COL_HI // V_WIDTH), vv(COL_HG // V_WIDTH),
                  pl.BlockSpec((DEPTH, QK_WIDTH), lambda bi, n: (0, 0)),
                  pl.BlockSpec((1, HEAD_V), lambda bi, n: (0, 0))],
        out_specs=pl.BlockSpec((1, c, V_WIDTH), lambda bi, n: (bi, n, 0)),
        scratch_shapes=[pltpu.VMEM((N_HEADS, HEAD_V, HEAD_K), F32)],
        compiler_params=_params(("arbitrary", "arbitrary")),
        name="hgrn2",
    )(proj3, proj3, proj3, proj3, lb_param.astype(F32), norm_g.astype(F32)[None, :])


def _merge_kernel(yr_ref, yg_ref, yh_ref, m0_ref, m1_ref, m2_ref, mb_ref, wb_ref, wo_ref, x_ref, lg_ref, lb_ref,
                  xo_ref, pa_ref, pb_ref):
    merged = None
    for n, (y_ref, m_ref) in enumerate(((yr_ref, m0_ref), (yg_ref, m1_ref), (yh_ref, m2_ref))):
        gate = jax.nn.sigmoid(m_ref[...].astype(F32) + mb_ref[:, n * D_MODEL:(n + 1) * D_MODEL])
        term = gate * _dot(y_ref[...], wb_ref[0, n])
        merged = term if merged is None else merged + term
    h = _dot(merged.astype(BF16), wo_ref[0])
    xn = _layer_norm(ALPHA * x_ref[...] + h, lg_ref[...], lb_ref[...])
    xo_ref[...] = xn
    pa_ref[...], pb_ref[...] = _pack_row(xn)


def _merge(y_ret, y_gdn, y_hg, proj, merge_b, wb, wo, layer, xf, ln_g, ln_b, tm=512):
    t = xf.shape[0]
    tm = min(tm, t)
    tile = lambda j: pl.BlockSpec((tm, D_MODEL), lambda i: (i, j))
    row = lambda wd: pl.BlockSpec((1, wd), lambda i: (0, 0))
    packed = pl.BlockSpec((tm, D_MODEL // 4), lambda i: (i, 0))
    packed_shape = jax.ShapeDtypeStruct((t, D_MODEL // 4), jnp.int32)
    return pl.pallas_call(
        _merge_kernel,
        out_shape=(jax.ShapeDtypeStruct((t, D_MODEL), F32), packed_shape, packed_shape),
        grid=(t // tm,),
        in_specs=[tile(0), tile(0), tile(0),
                  tile(COL_MG // D_MODEL), tile(COL_MG // D_MODEL + 1), tile(COL_MG // D_MODEL + 2),
                  row(N_BRANCH * D_MODEL),
                  pl.BlockSpec((1, N_BRANCH, V_WIDTH, D_MODEL), lambda i: (layer, 0, 0, 0)),
                  pl.BlockSpec((1, D_MODEL, D_MODEL), lambda i: (layer, 0, 0)),
                  tile(0), row(D_MODEL), row(D_MODEL)],
        out_specs=(tile(0), packed, packed),
        compiler_params=_params(("arbitrary",)),
        name="merge_out_ln",
    )(y_ret, y_gdn, y_hg, proj, proj, proj, merge_b.astype(F32)[None, :], wb, wo, xf,
      ln_g.astype(F32)[None, :], ln_b.astype(F32)[None, :])


def _router_kernel(x_ref, rw_ref, rb_ref, ct_ref, rk_ref, cnt_ref, sel_ref, carry_ref):
    @pl.when(pl.program_id(0) == 0)
    def _():
        carry_ref[...] = jnp.zeros_like(carry_ref)

    x = x_ref[...]
    w = rw_ref[...]
    xh = x.astype(BF16)
    xl = (x - xh.astype(F32)).astype(BF16)
    wh = w.astype(BF16)
    wl = (w - wh.astype(F32)).astype(BF16)
    logits = (_dot(xh, wh) + (_dot(xl, wh) + _dot(xh, wl))).T[:N_EXPERTS]
    scores = jax.nn.sigmoid(logits)
    biased = scores + rb_ref[...]
    epg = EXPERTS_PER_GROUP
    brow = [biased[e:e + 1] for e in range(N_EXPERTS)]
    srow = [scores[e:e + 1] for e in range(N_EXPERTS)]
    gscore = []
    for g in range(N_GROUPS):
        a, b, c, d = brow[g * epg:(g + 1) * epg]
        hi1, lo1, hi2, lo2 = jnp.maximum(a, b), jnp.minimum(a, b), jnp.maximum(c, d), jnp.minimum(c, d)
        gscore.append(jnp.maximum(hi1, hi2) + jnp.maximum(jnp.minimum(hi1, hi2), jnp.maximum(lo1, lo2)))
    best = jnp.zeros(gscore[0].shape, jnp.int32)
    top = gscore[0]
    for g in range(1, N_GROUPS):
        upd = gscore[g] > top
        best = jnp.where(upd, g, best)
        top = jnp.where(upd, gscore[g], top)

    def pick(rows, j):
        out = rows[(N_GROUPS - 1) * epg + j]
        for g in range(N_GROUPS - 2, -1, -1):
            out = jnp.where(best == g, rows[g * epg + j], out)
        return out

    bv = [pick(brow, j) for j in range(epg)]
    sv = [pick(srow, j) for j in range(epg)]
    chosen = []
    for j in range(epg):
        rank = jnp.zeros(best.shape, jnp.int32)
        for i in range(epg):
            if i == j:
                continue
            ahead = (bv[i] > bv[j]) if i > j else (bv[i] >= bv[j])
            rank = rank + ahead.astype(jnp.int32)
        chosen.append(rank < 2)
    den = None
    for j in range(epg):
        term = jnp.where(chosen[j], sv[j], 0.0)
        den = term if den is None else den + term
    for g in range(N_GROUPS):
        for j in range(epg):
            e = g * epg + j
            hit = jnp.logical_and(chosen[j], best == g)
            ct_ref[e:e + 1, :] = jnp.where(hit, sv[j] / den, 0.0)
            sel_ref[e:e + 1, :] = jnp.where(hit, 1.0, 0.0)
    sel = sel_ref[...]
    tm = sel.shape[1]
    before = lax.broadcasted_iota(jnp.int32, (tm, tm), 0) < lax.broadcasted_iota(jnp.int32, (tm, tm), 1)
    carry = carry_ref[:, 0:1]
    rank = _dot(sel.astype(BF16), jnp.where(before, 1.0, 0.0).astype(BF16)) + carry
    rk_ref[...] = jnp.where(sel > 0.0, rank, -1.0).astype(jnp.int32)
    total = jnp.broadcast_to(carry + jnp.sum(sel, axis=1, keepdims=True), carry_ref.shape)
    carry_ref[...] = total
    cnt_ref[...] = total


def _router(xf, router_w, router_b, tm=1024):
    t = xf.shape[0]
    tm = min(tm, t)
    col = pl.BlockSpec((N_EXPERTS, tm), lambda i: (0, i))
    return pl.pallas_call(
        _router_kernel,
        out_shape=(jax.ShapeDtypeStruct((N_EXPERTS, t), F32), jax.ShapeDtypeStruct((N_EXPERTS, t), jnp.int32),
                   jax.ShapeDtypeStruct((N_EXPERTS, LANES), F32)),
        grid=(t // tm,),
        in_specs=[pl.BlockSpec((tm, D_MODEL), lambda i: (i, 0)),
                  pl.BlockSpec((D_MODEL, LANES), lambda i: (0, 0)),
                  pl.BlockSpec((N_EXPERTS, 1), lambda i: (0, 0))],
        out_specs=(col, col, pl.BlockSpec((N_EXPERTS, LANES), lambda i: (0, 0))),
        scratch_shapes=[pltpu.VMEM((N_EXPERTS, tm), F32), pltpu.VMEM((N_EXPERTS, LANES), F32)],
        compiler_params=_params(("arbitrary",)),
        name="router",
    )(xf, jnp.pad(router_w.astype(F32), ((0, 0), (0, LANES - N_EXPERTS))), router_b.astype(F32)[:, None])


def _sc_mesh():
    return plsc.VectorSubcoreMesh(core_axis_name="core", subcore_axis_name="subcore",
                                  num_cores=SC_CORES, num_subcores=SC_SUBCORES)


def _sc_scatter_rows(src, idx0, idx1, n_out):
    n, w = src.shape

    @pl.kernel(out_type=jax.ShapeDtypeStruct((n_out, w), src.dtype), mesh=_sc_mesh(), scratch_types=[],
               name="dispatch_rows")
    def run(x_hbm, i0_hbm, i1_hbm, o_hbm):
        def body(x_vmem, i0_vmem, i1_vmem):
            pltpu.sync_copy(x_vmem, o_hbm.at[i0_vmem.at[0]])
            pltpu.sync_copy(x_vmem, o_hbm.at[i1_vmem.at[0]])

        pltpu.emit_pipeline(
            body, grid=(n // SC_WINDOW,),
            in_specs=[pl.BlockSpec((SC_WINDOW, w), index_map=lambda i: (i, 0)),
                      pl.BlockSpec((1, SC_WINDOW), index_map=lambda i: (0, i)),
                      pl.BlockSpec((1, SC_WINDOW), index_map=lambda i: (0, i))],
            out_specs=[], core_axis_name=("core", "subcore"), dimension_semantics=(pltpu.PARALLEL,),
        )(x_hbm, i0_hbm, i1_hbm)

    return run(src, idx0.reshape(1, n), idx1.reshape(1, n))


def _sc_gather_rows(src, idx):
    n = idx.shape[0]
    w = src.shape[1]

    @pl.kernel(out_type=jax.ShapeDtypeStruct((n, w), src.dtype), mesh=_sc_mesh(), scratch_types=[],
               name="collect_rows")
    def run(x_hbm, i_hbm, o_hbm):
        def body(i_vmem, o_vmem):
            pltpu.sync_copy(x_hbm.at[i_vmem.at[0]], o_vmem)

        pltpu.emit_pipeline(
            body, grid=(n // SC_WINDOW,),
            in_specs=[pl.BlockSpec((1, SC_WINDOW), index_map=lambda i: (0, i))],
            out_specs=[pl.BlockSpec((SC_WINDOW, w), index_map=lambda i: (i, 0))],
            core_axis_name=("core", "subcore"), dimension_semantics=(pltpu.PARALLEL,),
        )(i_hbm, o_hbm)

    return run(src, idx.reshape(1, n))


def _dispatch_plan(rank_t, comb_t, counts):
    tg = EXPERT_TILE
    t = rank_t.shape[1]
    n_tiles = 2 * t // tg + N_EXPERTS
    cnt = counts[:, 0].astype(jnp.int32)
    tiles = (cnt + tg - 1) // tg
    tile_end = jnp.cumsum(tiles)
    seg_start = (tile_end - tiles) * tg
    n_active = tile_end[-1]
    tile_ids = jnp.arange(n_tiles, dtype=jnp.int32)
    tile_expert = jnp.sum(tile_end[None, :] <= jnp.minimum(tile_ids, n_active - 1)[:, None], axis=1).astype(jnp.int32)
    routed = rank_t >= 0
    pos = seg_start[:, None] + rank_t
    order = jnp.cumsum(routed.astype(jnp.int32), axis=0)
    first = jnp.logical_and(routed, order == 1)
    second = jnp.logical_and(routed, order == 2)
    dest0 = jnp.sum(jnp.where(first, pos, 0), axis=0)
    dest1 = jnp.sum(jnp.where(second, pos, 0), axis=0)
    w01 = jnp.stack([jnp.sum(jnp.where(first, comb_t, 0.0), axis=0), jnp.sum(jnp.where(second, comb_t, 0.0), axis=0)], axis=1)
    return dest0, dest1, w01, tile_expert, n_active.reshape(1).astype(jnp.int32), n_tiles * tg


def _expert_kernel(te_ref, na_ref, xa_ref, xb_ref, wg_ref, wu_ref, wd_ref, ya_ref, yb_ref, wgb_ref, wub_ref, wdb_ref):
    i = pl.program_id(0)
    live = i < na_ref[0]

    @pl.when(jnp.logical_or(i == 0, te_ref[i] != te_ref[jnp.maximum(i - 1, 0)]))
    def _():
        wgb_ref[...] = wg_ref[0, 0].astype(BF16)
        wub_ref[...] = wu_ref[0, 0].astype(BF16)
        wdb_ref[...] = wd_ref[0, 0].astype(BF16)

    @pl.when(live)
    def _():
        x = _unpack_row(xa_ref[...], xb_ref[...]).astype(BF16)
        h = jax.nn.silu(_dot(x, wgb_ref[...])) * _dot(x, wub_ref[...])
        ya_ref[...], yb_ref[...] = _pack_row(_dot(h.astype(BF16), wdb_ref[...]))

    @pl.when(jnp.logical_not(live))
    def _():
        ya_ref[...] = jnp.zeros_like(ya_ref)
        yb_ref[...] = jnp.zeros_like(yb_ref)


def _experts(xsa, xsb, wg, wu, wd, layer, tile_expert, n_active):
    n, q = xsa.shape
    tg = EXPERT_TILE
    rows = pl.BlockSpec((tg, q), lambda i, te, na: (i, 0))
    grid_spec = pltpu.PrefetchScalarGridSpec(
        num_scalar_prefetch=2, grid=(n // tg,),
        in_specs=[rows, rows,
                  pl.BlockSpec((1, 1, D_MODEL, D_EXPERT), lambda i, te, na: (layer, te[i], 0, 0)),
                  pl.BlockSpec((1, 1, D_MODEL, D_EXPERT), lambda i, te, na: (layer, te[i], 0, 0)),
                  pl.BlockSpec((1, 1, D_EXPERT, D_MODEL), lambda i, te, na: (layer, te[i], 0, 0))],
        out_specs=(rows, rows),
        scratch_shapes=[pltpu.VMEM((D_MODEL, D_EXPERT), BF16), pltpu.VMEM((D_MODEL, D_EXPERT), BF16),
                        pltpu.VMEM((D_EXPERT, D_MODEL), BF16)])
    shape = jax.ShapeDtypeStruct((n, q), jnp.int32)
    return pl.pallas_call(
        _expert_kernel, grid_spec=grid_spec, out_shape=(shape, shape),
        compiler_params=_params(("arbitrary",)), name="experts",
    )(tile_expert, n_active, xsa, xsb, wg, wu, wd)


def _combine_kernel(g0a_ref, g0b_ref, g1a_ref, g1b_ref, w_ref, x_ref, lg_ref, lb_ref, xo_ref, xbo_ref):
    y0 = _unpack_row(g0a_ref[...], g0b_ref[...])
    y1 = _unpack_row(g1a_ref[...], g1b_ref[...])
    w = w_ref[...]
    xn = _layer_norm(ALPHA * x_ref[...] + (w[:, 0:1] * y0 + w[:, 1:2] * y1), lg_ref[...], lb_ref[...])
    xo_ref[...] = xn
    xbo_ref[...] = xn.astype(BF16)


def _combine(g0a, g0b, g1a, g1b, w01, xf, ln_g, ln_b, tm=1024):
    t = xf.shape[0]
    tm = min(tm, t)
    tile = pl.BlockSpec((tm, D_MODEL), lambda i: (i, 0))
    packed = pl.BlockSpec((tm, D_MODEL // 4), lambda i: (i, 0))
    row = pl.BlockSpec((1, D_MODEL), lambda i: (0, 0))
    return pl.pallas_call(
        _combine_kernel,
        out_shape=(jax.ShapeDtypeStruct((t, D_MODEL), F32), jax.ShapeDtypeStruct((t, D_MODEL), BF16)),
        grid=(t // tm,),
        in_specs=[packed, packed, packed, packed, pl.BlockSpec((tm, 2), lambda i: (i, 0)), tile, row, row],
        out_specs=(tile, tile),
        compiler_params=_params(("arbitrary",)),
        name="combine_ln",
    )(g0a, g0b, g1a, g1b, w01, xf, ln_g.astype(F32)[None, :], ln_b.astype(F32)[None, :])


def kernel(x, positions, w_in, gdn_conv_w, gdn_a_log, gdn_dt_bias, gdn_norm_g, hgrn_lb, hgrn_norm_g, merge_b,
           w_branch, w_out, ln1_g, ln1_b, router_w, router_b, moe_w_gate, moe_w_up, moe_w_down, ln2_g, ln2_b):
    b, s, d = x.shape
    t = b * s
    assert d == D_MODEL and w_in.shape[-1] == 2 * HALF_WIDTH + SMALL_WIDTH and SMALL_START == HALF_WIDTH

    w_lo = w_in[:, :, :HALF_WIDTH].astype(BF16)
    w_hi = w_in[:, :, SMALL_START + SMALL_WIDTH:].astype(BF16)
    w_small = jnp.pad(w_in[:, :, SMALL_START:SMALL_START + SMALL_WIDTH], ((0, 0), (0, 0), (0, LANES - SMALL_WIDTH))).astype(BF16)
    wb = w_branch.astype(BF16)
    wo = w_out.astype(BF16)
    wg, wu, wd = moe_w_gate, moe_w_up, moe_w_down

    cos, sin = _rope_tables(positions)
    cos3 = cos.reshape(b, s, HEAD_K)
    sin3 = sin.reshape(b, s, HEAD_K)

    xf = x.reshape(t, d).astype(F32)
    xb = xf.astype(BF16)
    for l in range(DEPTH):
        proj_lo, small = _in_proj(xb, w_lo, l, HALF_WIDTH, w_small=w_small)
        proj_hi = _in_proj(xb, w_hi, l, HALF_WIDTH)
        lo3 = proj_lo.reshape(b, s, HALF_WIDTH)
        hi3 = proj_hi.reshape(b, s, HALF_WIDTH)
        y_ret = _retention(lo3, cos3, sin3)
        y_gdn = _gdn(lo3, small.reshape(b, s, LANES), gdn_conv_w[l], gdn_a_log[l], gdn_dt_bias[l], gdn_norm_g[l])
        y_hg = _hgrn2(hi3, hgrn_lb, hgrn_norm_g[l], l)
        xf, pa, pb = _merge(y_ret.reshape(t, V_WIDTH), y_gdn.reshape(t, V_WIDTH), y_hg.reshape(t, V_WIDTH), proj_hi,
                            merge_b[l], wb, wo, l, xf, ln1_g[l], ln1_b[l])
        comb_t, rank_t, counts = _router(xf, router_w, router_b)
        dest0, dest1, w01, tile_expert, n_active, n_rows = _dispatch_plan(rank_t, comb_t, counts)
        ysa, ysb = _experts(_sc_scatter_rows(pa, dest0, dest1, n_rows), _sc_scatter_rows(pb, dest0, dest1, n_rows),
                            wg, wu, wd, l, tile_expert, n_active)
        xf, xb = _combine(_sc_gather_rows(ysa, dest0), _sc_gather_rows(ysb, dest0),
                          _sc_gather_rows(ysa, dest1), _sc_gather_rows(ysb, dest1), w01, xf, ln2_g[l], ln2_b[l])
    return xf.reshape(b, s, d).astype(x.dtype)
```

```python
import functools
import math

import jax
import jax.numpy as jnp
from jax import lax
from jax.experimental import pallas as pl
from jax.experimental.pallas import tpu as pltpu
from jax.experimental.pallas import tpu_sc as plsc

F32 = jnp.float32
BF16 = jnp.bfloat16

D_MODEL = 1024
DEPTH = 4
N_HEADS = 4
HEAD_V = 256
HEAD_K = 128
QK_WIDTH = N_HEADS * HEAD_K
V_WIDTH = N_HEADS * HEAD_V
N_BRANCH = 3
ROPE_BASE = 10000.0
RET_DECAY_OFFSET = 5.0
N_EXPERTS = 16
N_GROUPS = 4
EXPERTS_PER_GROUP = N_EXPERTS // N_GROUPS
D_EXPERT = D_MODEL // 2
ALPHA = (2 * DEPTH) ** 0.25
LN_EPS = 1e-5
NORM_EPS = 1e-6

SMALL_START = 2 * QK_WIDTH + 2 * V_WIDTH + (2 * QK_WIDTH + V_WIDTH) + V_WIDTH
SMALL_WIDTH = 2 * N_HEADS
HALF_WIDTH = 6144
COL_RQ, COL_RK, COL_RV, COL_RG = 0, 512, 1024, 2048
COL_GQ, COL_GK, COL_GV, COL_GZ = 3072, 3584, 4096, 5120
COL_HQ, COL_HF, COL_HI, COL_HG = 0, 512, 1024, 2048
COL_MG = 3072

LANES = 128
SUBLANES = 8
VMEM_LIMIT = 56 * 1024 * 1024

RET_CHUNK = 256
CHUNK = 64
STEP_TOKENS = 512
HGRN_SUB = 16
EXPERT_TILE = 512
COMBINE_PARTS = 2
SC_CORES = 2
SC_SUBCORES = 16
SC_WINDOW = 128


def _dot(a, b):
    return jnp.dot(a, b, preferred_element_type=F32)


def _dot_nt(a, b):
    return lax.dot_general(a, b, (((1,), (1,)), ((), ())), preferred_element_type=F32)


def _dot_tn(a, b):
    return lax.dot_general(a, b, (((0,), (0,)), ((), ())), preferred_element_type=F32)


def _cumsum_rows(x, seg):
    assert seg & (seg - 1) == 0
    pos = lax.broadcasted_iota(jnp.int32, x.shape, 0) & (seg - 1)
    s = 1
    while s < seg:
        x = x + jnp.where(pos >= s, pltpu.roll(x, s, axis=0), 0.0)
        s *= 2
    return x


def _pack_pair(a, b):
    ua = lax.bitcast_convert_type(a.astype(BF16).astype(F32), jnp.int32)
    ub = lax.bitcast_convert_type(b.astype(BF16).astype(F32), jnp.int32)
    return lax.shift_right_logical(ua, 16) | (ub & jnp.int32(-65536))


def _unpack_pair(w):
    a = lax.bitcast_convert_type(lax.shift_left(w, 16), F32)
    b = lax.bitcast_convert_type(w & jnp.int32(-65536), F32)
    return a, b


def _pack_row(x):
    q = D_MODEL // 4
    return _pack_pair(x[:, :q], x[:, q:2 * q]), _pack_pair(x[:, 2 * q:3 * q], x[:, 3 * q:])


def _unpack_row(pa, pb):
    return jnp.concatenate(_unpack_pair(pa) + _unpack_pair(pb), axis=1)


def _layer_norm(z, g, b):
    mu = jnp.mean(z, axis=-1, keepdims=True)
    zc = z - mu
    var = jnp.mean(zc * zc, axis=-1, keepdims=True)
    return zc * lax.rsqrt(var + LN_EPS) * g + b


def _params(sem):
    return pltpu.CompilerParams(dimension_semantics=sem, vmem_limit_bytes=VMEM_LIMIT)


def _rope_kernel(pos_ref, invf_ref, cos_ref, sin_ref):
    ang = pos_ref[...].astype(F32) * invf_ref[...]
    lane = lax.broadcasted_iota(jnp.int32, ang.shape, 1)
    cos_ref[...] = jnp.cos(ang)
    s = jnp.sin(ang)
    sin_ref[...] = jnp.where(lane < HEAD_K // 2, -s, s)


def _rope_tables(positions):
    t = positions.size
    half = HEAD_K // 2
    inv_freq = 1.0 / (ROPE_BASE ** jnp.linspace(0.0, 1.0, half, dtype=F32))
    invf = jnp.concatenate([inv_freq, inv_freq])[None, :]
    ts = min(t, 2048)
    return pl.pallas_call(
        _rope_kernel,
        out_shape=(jax.ShapeDtypeStruct((t, HEAD_K), F32), jax.ShapeDtypeStruct((t, HEAD_K), F32)),
        grid=(t // ts,),
        in_specs=[pl.BlockSpec((ts, 1), lambda i: (i, 0)), pl.BlockSpec((1, HEAD_K), lambda i: (0, 0))],
        out_specs=(pl.BlockSpec((ts, HEAD_K), lambda i: (i, 0)), pl.BlockSpec((ts, HEAD_K), lambda i: (i, 0))),
        compiler_params=_params(("arbitrary",)),
        name="rope_tables",
    )(positions.reshape(t, 1), invf)


def _matmul_kernel(x_ref, w_ref, o_ref):
    o_ref[...] = _dot(x_ref[...], w_ref[0]).astype(o_ref.dtype)


def _matmul_small_kernel(x_ref, w_ref, ws_ref, o_ref, s_ref):
    x = x_ref[...]
    o_ref[...] = _dot(x, w_ref[0]).astype(o_ref.dtype)

    @pl.when(pl.program_id(1) == 0)
    def _():
        s_ref[...] = _dot(x, ws_ref[0])


def _in_proj(xb, w, layer, n, w_small=None, tm=2048, tn=1536):
    t, d = xb.shape
    tm = min(tm, t)
    x_spec = pl.BlockSpec((tm, d), lambda i, j: (i, 0))
    w_spec = pl.BlockSpec((1, d, tn), lambda i, j: (layer, 0, j))
    o_spec = pl.BlockSpec((tm, tn), lambda i, j: (i, j))
    o_shape = jax.ShapeDtypeStruct((t, n), BF16)
    if w_small is None:
        kern, in_specs, out_specs, out_shape, args = _matmul_kernel, [x_spec, w_spec], o_spec, o_shape, (xb, w)
    else:
        kern = _matmul_small_kernel
        in_specs = [x_spec, w_spec, pl.BlockSpec((1, d, LANES), lambda i, j: (layer, 0, 0))]
        out_specs = (o_spec, pl.BlockSpec((tm, LANES), lambda i, j: (i, 0)))
        out_shape = (o_shape, jax.ShapeDtypeStruct((t, LANES), F32))
        args = (xb, w, w_small)
    return pl.pallas_call(
        kern, out_shape=out_shape, grid=(t // tm, n // tn), in_specs=in_specs, out_specs=out_specs,
        compiler_params=_params(("arbitrary", "arbitrary")), name="in_proj",
    )(*args)


def _retention_kernel(q_ref, k_ref, v_ref, g_ref, cos_ref, sin_ref, o_ref, state_ref, intra_ref, qdec_ref, kdec_ref,
                      *, chunk):
    @pl.when(pl.program_id(1) == 0)
    def _():
        state_ref[...] = jnp.zeros_like(state_ref)

    c = chunk
    log_gamma = [math.log1p(-(2.0 ** (-RET_DECAY_OFFSET - h))) for h in range(N_HEADS)]

    @pl.when(jnp.logical_and(pl.program_id(0) == 0, pl.program_id(1) == 0))
    def _():
        t_col = lax.broadcasted_iota(jnp.int32, (c, HEAD_K), 0).astype(F32)
        rel = (lax.broadcasted_iota(jnp.int32, (c, c), 0) - lax.broadcasted_iota(jnp.int32, (c, c), 1)).astype(F32)
        for h, lg in enumerate(log_gamma):
            intra_ref[h] = jnp.where(rel >= 0, jnp.exp(lg * rel), 0.0) * (HEAD_K ** -0.5)
            qdec_ref[h] = jnp.exp(lg * (t_col + 1.0))
            kdec_ref[h] = jnp.exp(lg * (c - 1.0 - t_col)) * (HEAD_K ** -0.5)

    cos = cos_ref[0]
    sin = sin_ref[0]
    for h, lg in enumerate(log_gamma):
        q = q_ref[0, :, h * HEAD_K:(h + 1) * HEAD_K].astype(F32)
        k = k_ref[0, :, h * HEAD_K:(h + 1) * HEAD_K].astype(F32)
        q = q * cos + pltpu.roll(q, HEAD_K // 2, axis=1) * sin
        k = k * cos + pltpu.roll(k, HEAD_K // 2, axis=1) * sin
        v = v_ref[0, :, h * HEAD_V:(h + 1) * HEAD_V]
        scores = _dot_nt(q.astype(BF16), k.astype(BF16)) * intra_ref[h]
        st = state_ref[h]
        out = _dot(scores.astype(BF16), v) + _dot((q * qdec_ref[h]).astype(BF16), st.astype(BF16))
        state_ref[h] = math.exp(lg * c) * st + _dot_tn((k * kdec_ref[h]).astype(BF16), v)
        mu = jnp.mean(out, axis=-1, keepdims=True)
        oc = out - mu
        var = jnp.mean(oc * oc, axis=-1, keepdims=True)
        gate = g_ref[0, :, h * HEAD_V:(h + 1) * HEAD_V].astype(F32)
        o_ref[0, :, h * HEAD_V:(h + 1) * HEAD_V] = (jax.nn.silu(gate) * oc * lax.rsqrt(var + NORM_EPS)).astype(o_ref.dtype)


def _retention(proj3, cos3, sin3):
    b, s, _ = proj3.shape
    c = min(RET_CHUNK, s)
    qk = lambda j: pl.BlockSpec((1, c, QK_WIDTH), lambda bi, n: (bi, n, j))
    vv = lambda j: pl.BlockSpec((1, c, V_WIDTH), lambda bi, n: (bi, n, j))
    tab = pl.BlockSpec((1, c, HEAD_K), lambda bi, n: (bi, n, 0))
    return pl.pallas_call(
        functools.partial(_retention_kernel, chunk=c),
        out_shape=jax.ShapeDtypeStruct((b, s, V_WIDTH), BF16),
        grid=(b, s // c),
        in_specs=[qk(COL_RQ // QK_WIDTH), qk(COL_RK // QK_WIDTH), vv(COL_RV // V_WIDTH), vv(COL_RG // V_WIDTH), tab, tab],
        out_specs=pl.BlockSpec((1, c, V_WIDTH), lambda bi, n: (bi, n, 0)),
        scratch_shapes=[pltpu.VMEM((N_HEADS, HEAD_K, HEAD_V), F32), pltpu.VMEM((N_HEADS, c, c), F32),
                        pltpu.VMEM((N_HEADS, c, HEAD_K), F32), pltpu.VMEM((N_HEADS, c, HEAD_K), F32)],
        compiler_params=_params(("arbitrary", "arbitrary")),
        name="retention",
    )(proj3, proj3, proj3, proj3, cos3, sin3)


def _conv_silu(x, tail_ref, w):
    c = x.shape[0]
    prev = tail_ref[...]
    row8 = lax.broadcasted_iota(jnp.int32, (SUBLANES, 1), 0)
    y = w[3:4] * x
    for j in (1, 2, 3):
        xs = pltpu.roll(x, j, axis=0)
        head = jnp.where(row8 >= j, xs[:SUBLANES], pltpu.roll(prev, j, axis=0))
        xs = jnp.concatenate([head, xs[SUBLANES:]], axis=0)
        y = y + w[3 - j:4 - j] * xs
    tail_ref[...] = x[c - SUBLANES:]
    return y * jax.nn.sigmoid(y)


def _unit_lower_inverses(mats, order):
    n = mats[0].shape[0]
    eye = (lax.broadcasted_iota(jnp.int32, (n, n), 0) == lax.broadcasted_iota(jnp.int32, (n, n), 1)).astype(F32)
    ps = [-a for a in mats]
    invs = [eye + p for p in ps]
    k = 1
    while 2 * k < order:
        pbs = [p.astype(BF16) for p in ps]
        ps = [_dot(pb, pb) for pb in pbs]
        invs = [inv + _dot(inv.astype(BF16), p.astype(BF16)) for inv, p in zip(invs, ps)]
        k *= 2
    return invs


def _stack_heads(x, rows, width):
    return jnp.concatenate([x[rows, h * width:(h + 1) * width] for h in range(N_HEADS)], axis=0)


def _gdn_kernel(qp_ref, kp_ref, vp_ref, z_ref, sm_ref, cwq_ref, cwk_ref, cwv_ref, alog_ref, dtb_ref, gn_ref,
                o_ref, state_ref, tq_ref, tk_ref, tv_ref, qs_ref, os_ref, ms_ref, ns_ref, *, n_chunks):
    @pl.when(pl.program_id(1) == 0)
    def _():
        state_ref[...] = jnp.zeros_like(state_ref)
        tq_ref[...] = jnp.zeros_like(tq_ref)
        tk_ref[...] = jnp.zeros_like(tk_ref)
        tv_ref[...] = jnp.zeros_like(tv_ref)

    c = CHUNK
    hc = N_HEADS * c
    q_all = _conv_silu(qp_ref[0].astype(F32), tq_ref, cwq_ref[...])
    k_all = _conv_silu(kp_ref[0].astype(F32), tk_ref, cwk_ref[...])
    v_all = _conv_silu(vp_ref[0].astype(F32), tv_ref, cwv_ref[...])
    sm = sm_ref[0]
    beta_all = jax.nn.sigmoid(sm)
    g_all = -jnp.exp(alog_ref[...]) * jax.nn.softplus(sm + dtb_ref[...])
    gc_all = _cumsum_rows(g_all, c)
    gc_t = gc_all.T
    ri = lax.broadcasted_iota(jnp.int32, (hc, hc), 0)
    ci = lax.broadcasted_iota(jnp.int32, (hc, hc), 1)
    shift = c.bit_length() - 1
    lower = jnp.logical_and(ri >> shift == ci >> shift, ri >= ci)
    diag = ri == ci
    row_head = lax.broadcasted_iota(jnp.int32, (hc, 1), 0) >> shift
    col = lambda x, rows, j: jnp.concatenate([x[rows, j + h:j + h + 1] for h in range(N_HEADS)], axis=0)

    pre = []
    for j in range(n_chunks):
        rows = slice(j * c, (j + 1) * c)
        kraw = _stack_heads(k_all, rows, HEAD_K)
        qraw = _stack_heads(q_all, rows, HEAD_K)
        kn = kraw * lax.rsqrt(jnp.sum(kraw * kraw, axis=-1, keepdims=True) + NORM_EPS)
        qn = qraw * lax.rsqrt(jnp.sum(qraw * qraw, axis=-1, keepdims=True) + NORM_EPS) * (HEAD_K ** -0.5)
        beta = col(beta_all, rows, 0)
        gc = col(gc_all, rows, N_HEADS)
        gr = jnp.concatenate([gc_t[N_HEADS + h:N_HEADS + h + 1, rows] for h in range(N_HEADS)], axis=1)
        last = slice((j + 1) * c - 1, (j + 1) * c)
        g_last = [gc_all[last, N_HEADS + h:N_HEADS + h + 1] for h in range(N_HEADS)]
        gl = jnp.concatenate([jnp.broadcast_to(g, (c, 1)) for g in g_last], axis=0)
        pre.append(dict(kn=kn, qn=qn, kb=kn * beta, vb=_stack_heads(v_all, rows, HEAD_V) * beta, gc=gc, gr=gr,
                        eg=jnp.exp(gc), gl=gl, egl=[jnp.exp(g) for g in g_last]))
    decays = [jnp.where(lower, jnp.exp(p["gc"] - p["gr"]), 0.0) for p in pre]
    kqs = [_dot_nt(jnp.concatenate([p["kb"], p["qn"]], axis=0).astype(BF16), p["kn"].astype(BF16)) for p in pre]
    tms = _unit_lower_inverses([jnp.where(diag, 0.0, kq[:hc] * d) for kq, d in zip(kqs, decays)], c)
    uws = [_dot(tm.astype(BF16), jnp.concatenate([p["vb"], p["kb"] * p["eg"]], axis=1).astype(BF16)).astype(BF16)
           for tm, p in zip(tms, pre)]
    aos = [_dot((kq[hc:] * d).astype(BF16), uw) for kq, d, uw in zip(kqs, decays, uws)]
    for j, (p, uw, ao) in enumerate(zip(pre, uws, aos)):
        kd = p["kn"] * jnp.exp(p["gl"] - p["gc"])
        kd_wide = jnp.concatenate([jnp.where(row_head == h, kd, 0.0) for h in range(N_HEADS)], axis=1)
        nm = _dot_tn(kd_wide.astype(BF16), uw)
        qs_ref[j] = (p["qn"] * p["eg"] - ao[:, HEAD_V:]).astype(BF16)
        os_ref[j] = ao[:, :HEAD_V]
        ns_ref[j] = nm[:, :HEAD_V]
        ms_ref[j] = nm[:, HEAD_V:].astype(BF16)

    for j in range(n_chunks):
        sts = [state_ref[h] for h in range(N_HEADS)]
        rrs = [_dot(jnp.concatenate([qs_ref[j, h * c:(h + 1) * c], ms_ref[j, h * HEAD_K:(h + 1) * HEAD_K]], axis=0),
                    sts[h].astype(BF16)) for h in range(N_HEADS)]
        for h in range(N_HEADS):
            state_ref[h] = pre[j]["egl"][h] * sts[h] - rrs[h][c:] + ns_ref[j, h * HEAD_K:(h + 1) * HEAD_K]
        for h in range(N_HEADS):
            out = rrs[h][:c] + os_ref[j, h * c:(h + 1) * c]
            on = out * lax.rsqrt(jnp.mean(out * out, axis=-1, keepdims=True) + NORM_EPS) * gn_ref[...]
            z = z_ref[0, j * c:(j + 1) * c, h * HEAD_V:(h + 1) * HEAD_V].astype(F32)
            o_ref[0, j * c:(j + 1) * c, h * HEAD_V:(h + 1) * HEAD_V] = (jax.nn.silu(z) * on).astype(o_ref.dtype)


def _lane_row(vals, start):
    return jnp.zeros((1, LANES), F32).at[0, start:start + vals.shape[0]].set(vals.astype(F32))


def _gdn(proj3, small3, conv_w, a_log, dt_bias, norm_g):
    b, s, _ = proj3.shape
    c = min(STEP_TOKENS, s)
    g = c // CHUNK
    hc = N_HEADS * CHUNK
    qk = lambda j: pl.BlockSpec((1, c, QK_WIDTH), lambda bi, n: (bi, n, j))
    vv = lambda j: pl.BlockSpec((1, c, V_WIDTH), lambda bi, n: (bi, n, j))
    cw = lambda wd, j: pl.BlockSpec((4, wd), lambda bi, n: (0, j))
    row = lambda wd: pl.BlockSpec((1, wd), lambda bi, n: (0, 0))
    conv_w = conv_w.astype(F32)
    return pl.pallas_call(
        functools.partial(_gdn_kernel, n_chunks=g),
        out_shape=jax.ShapeDtypeStruct((b, s, V_WIDTH), BF16),
        grid=(b, s // c),
        in_specs=[qk(COL_GQ // QK_WIDTH), qk(COL_GK // QK_WIDTH), vv(COL_GV // V_WIDTH), vv(COL_GZ // V_WIDTH),
                  pl.BlockSpec((1, c, LANES), lambda bi, n: (bi, n, 0)),
                  cw(QK_WIDTH, 0), cw(QK_WIDTH, 1), cw(V_WIDTH, 1),
                  row(LANES), row(LANES), row(HEAD_V)],
        out_specs=pl.BlockSpec((1, c, V_WIDTH), lambda bi, n: (bi, n, 0)),
        scratch_shapes=[pltpu.VMEM((N_HEADS, HEAD_K, HEAD_V), F32),
                        pltpu.VMEM((SUBLANES, QK_WIDTH), F32), pltpu.VMEM((SUBLANES, QK_WIDTH), F32),
                        pltpu.VMEM((SUBLANES, V_WIDTH), F32),
                        pltpu.VMEM((g, hc, HEAD_K), BF16), pltpu.VMEM((g, hc, HEAD_V), F32),
                        pltpu.VMEM((g, N_HEADS * HEAD_K, HEAD_K), BF16), pltpu.VMEM((g, N_HEADS * HEAD_K, HEAD_V), F32)],
        compiler_params=_params(("arbitrary", "arbitrary")),
        name="gated_delta",
    )(proj3, proj3, proj3, proj3, small3, conv_w, conv_w, conv_w,
      _lane_row(a_log, N_HEADS), _lane_row(dt_bias, N_HEADS), norm_g.astype(F32)[None, :])


def _hgrn2_kernel(q_ref, f_ref, i_ref, g_ref, lbp_ref, gn_ref, o_ref, state_ref, *, n_chunks, layer):
    @pl.when(pl.program_id(1) == 0)
    def _():
        state_ref[...] = jnp.zeros_like(state_ref)

    c = CHUNK
    p = lbp_ref[...]
    e = jnp.exp(p - jnp.max(p, axis=0, keepdims=True))
    sm = e / jnp.sum(e, axis=0, keepdims=True)
    lb = jnp.zeros((1, QK_WIDTH), F32)
    for i in range(1, layer + 1):
        lb = lb + sm[i:i + 1]
    hf = f_ref[0].astype(F32)
    log_f = jnp.logaddexp(jax.nn.log_sigmoid(hf), jnp.log(lb) + jax.nn.log_sigmoid(-hf))
    k_all = (1.0 - lb) * jax.nn.sigmoid(-hf)
    q_all = q_ref[0].astype(F32) * (HEAD_K ** -0.5)
    gcum = _cumsum_rows(log_f, c)
    sub = HGRN_SUB
    nsub = c // sub
    sub_shift = sub.bit_length() - 1
    lane = lax.broadcasted_iota(jnp.int32, (sub, c), 1)
    lane_t = lane & (sub - 1)
    lane_blk = lane >> sub_shift
    srow = lax.broadcasted_iota(jnp.int32, (sub, 1), 0)
    zeros = lambda n: jnp.zeros((n, HEAD_K), F32)
    for j in range(n_chunks):
        rows = slice(j * c, (j + 1) * c)
        for h in range(N_HEADS):
            g = gcum[rows, h * HEAD_K:(h + 1) * HEAD_K]
            q = q_all[rows, h * HEAD_K:(h + 1) * HEAD_K]
            k = k_all[rows, h * HEAD_K:(h + 1) * HEAD_K]
            v = i_ref[0, rows, h * HEAD_V:(h + 1) * HEAD_V]
            q_cols, k_cols = [], []
            for i in range(1, nsub):
                r0 = i * sub
                ref = g[r0 - 1:r0]
                qt = q[r0:r0 + sub] * jnp.exp(g[r0:r0 + sub] - ref)
                q_cols.append(jnp.concatenate([zeros(r0), qt] + ([zeros(c - r0 - sub)] if r0 + sub < c else []), axis=0))
                k_cols.append(jnp.concatenate([k[:r0] * jnp.exp(ref - g[:r0]), zeros(c - r0)], axis=0))
            off_t = _dot_nt(jnp.concatenate(k_cols, axis=1).astype(BF16),
                            jnp.concatenate(q_cols, axis=1).astype(BF16))
            blocks = []
            for i in range(nsub):
                r0 = i * sub
                gb = g[r0:r0 + sub]
                qb = q[r0:r0 + sub]
                kb = k[r0:r0 + sub]
                acc_t = jnp.zeros((sub, c), F32)
                for t in range(sub):
                    w = kb * jnp.exp(gb[t:t + 1] - gb) * qb[t:t + 1]
                    col = jnp.where(srow <= t, jnp.sum(w, axis=1, keepdims=True), 0.0)
                    acc_t = jnp.where(lane_t == t, col, acc_t)
                blocks.append(jnp.where(lane_blk == i, acc_t, off_t[r0:r0 + sub]))
            attn_t = jnp.concatenate(blocks, axis=0)
            st = state_ref[h]
            out = _dot_tn(attn_t.astype(BF16), v) + _dot_nt((q * jnp.exp(g)).astype(BF16), st.astype(BF16))
            g_last = g[c - 1:c]
            state_ref[h] = st * jnp.exp(g_last) + _dot_tn(v, (k * jnp.exp(g_last - g)).astype(BF16))
            on = out * lax.rsqrt(jnp.mean(out * out, axis=-1, keepdims=True) + NORM_EPS) * gn_ref[...]
            gate = g_ref[0, rows, h * HEAD_V:(h + 1) * HEAD_V].astype(F32)
            o_ref[0, rows, h * HEAD_V:(h + 1) * HEAD_V] = (jax.nn.sigmoid(gate) * on).astype(o_ref.dtype)


def _hgrn2(proj3, lb_param, norm_g, layer):
    b, s, _ = proj3.shape
    c = min(STEP_TOKENS, s)
    qk = lambda j: pl.BlockSpec((1, c, QK_WIDTH), lambda bi, n: (bi, n, j))
    vv = lambda j: pl.BlockSpec((1, c, V_WIDTH), lambda bi, n: (bi, n, j))
    return pl.pallas_call(
        functools.partial(_hgrn2_kernel, n_chunks=c // CHUNK, layer=layer),
        out_shape=jax.ShapeDtypeStruct((b, s, V_WIDTH), BF16),
        grid=(b, s // c),
        in_specs=[qk(COL_HQ // QK_WIDTH), qk(COL_HF // QK_WIDTH), vv(COL_HI // V_WIDTH), vv(COL_HG // V_WIDTH),
                  pl.BlockSpec((DEPTH, QK_WIDTH), lambda bi, n: (0, 0)),
                  pl.BlockSpec((1, HEAD_V), lambda bi, n: (0, 0))],
        out_specs=pl.BlockSpec((1, c, V_WIDTH), lambda bi, n: (bi, n, 0)),
        scratch_shapes=[pltpu.VMEM((N_HEADS, HEAD_V, HEAD_K), F32)],
        compiler_params=_params(("arbitrary", "arbitrary")),
        name="hgrn2",
    )(proj3, proj3, proj3, proj3, lb_param.astype(F32), norm_g.astype(F32)[None, :])


def _merge_kernel(yr_ref, yg_ref, yh_ref, m0_ref, m1_ref, m2_ref, mb_ref, wb_ref, wo_ref, x_ref, lg_ref, lb_ref,
                  xo_ref, pa_ref, pb_ref):
    merged = None
    for n, (y_ref, m_ref) in enumerate(((yr_ref, m0_ref), (yg_ref, m1_ref), (yh_ref, m2_ref))):
        gate = jax.nn.sigmoid(m_ref[...].astype(F32) + mb_ref[:, n * D_MODEL:(n + 1) * D_MODEL])
        term = gate * _dot(y_ref[...], wb_ref[0, n])
        merged = term if merged is None else merged + term
    h = _dot(merged.astype(BF16), wo_ref[0])
    xn = _layer_norm(ALPHA * x_ref[...] + h, lg_ref[...], lb_ref[...])
    xo_ref[...] = xn
    pa_ref[...], pb_ref[...] = _pack_row(xn)


def _merge(y_ret, y_gdn, y_hg, proj, merge_b, wb, wo, layer, xf, ln_g, ln_b, tm=512):
    t = xf.shape[0]
    tm = min(tm, t)
    tile = lambda j: pl.BlockSpec((tm, D_MODEL), lambda i: (i, j))
    row = lambda wd: pl.BlockSpec((1, wd), lambda i: (0, 0))
    packed = pl.BlockSpec((tm, D_MODEL // 4), lambda i: (i, 0))
    packed_shape = jax.ShapeDtypeStruct((t, D_MODEL // 4), jnp.int32)
    return pl.pallas_call(
        _merge_kernel,
        out_shape=(jax.ShapeDtypeStruct((t, D_MODEL), F32), packed_shape, packed_shape),
        grid=(t // tm,),
        in_specs=[tile(0), tile(0), tile(0),
                  tile(COL_MG // D_MODEL), tile(COL_MG // D_MODEL + 1), tile(COL_MG // D_MODEL + 2),
                  row(N_BRANCH * D_MODEL),
                  pl.BlockSpec((1, N_BRANCH, V_WIDTH, D_MODEL), lambda i: (layer, 0, 0, 0)),
                  pl.BlockSpec((1, D_MODEL, D_MODEL), lambda i: (layer, 0, 0)),
                  tile(0), row(D_MODEL), row(D_MODEL)],
        out_specs=(tile(0), packed, packed),
        compiler_params=_params(("arbitrary",)),
        name="merge_out_ln",
    )(y_ret, y_gdn, y_hg, proj, proj, proj, merge_b.astype(F32)[None, :], wb, wo, xf,
      ln_g.astype(F32)[None, :], ln_b.astype(F32)[None, :])


def _router_kernel(x_ref, rw_ref, rb_ref, ct_ref, rk_ref, cnt_ref, sel_ref, carry_ref):
    @pl.when(pl.program_id(0) == 0)
    def _():
        carry_ref[...] = jnp.zeros_like(carry_ref)

    x = x_ref[...]
    w = rw_ref[...]
    xh = x.astype(BF16)
    xl = (x - xh.astype(F32)).astype(BF16)
    wh = w.astype(BF16)
    wl = (w - wh.astype(F32)).astype(BF16)
    logits = (_dot(xh, wh) + (_dot(xl, wh) + _dot(xh, wl))).T[:N_EXPERTS]
    scores = jax.nn.sigmoid(logits)
    biased = scores + rb_ref[...]
    epg = EXPERTS_PER_GROUP
    brow = [biased[e:e + 1] for e in range(N_EXPERTS)]
    srow = [scores[e:e + 1] for e in range(N_EXPERTS)]
    gscore = []
    for g in range(N_GROUPS):
        a, b, c, d = brow[g * epg:(g + 1) * epg]
        hi1, lo1, hi2, lo2 = jnp.maximum(a, b), jnp.minimum(a, b), jnp.maximum(c, d), jnp.minimum(c, d)
        gscore.append(jnp.maximum(hi1, hi2) + jnp.maximum(jnp.minimum(hi1, hi2), jnp.maximum(lo1, lo2)))
    best = jnp.zeros(gscore[0].shape, jnp.int32)
    top = gscore[0]
    for g in range(1, N_GROUPS):
        upd = gscore[g] > top
        best = jnp.where(upd, g, best)
        top = jnp.where(upd, gscore[g], top)

    def pick(rows, j):
        out = rows[(N_GROUPS - 1) * epg + j]
        for g in range(N_GROUPS - 2, -1, -1):
            out = jnp.where(best == g, rows[g * epg + j], out)
        return out

    bv = [pick(brow, j) for j in range(epg)]
    sv = [pick(srow, j) for j in range(epg)]
    chosen = []
    for j in range(epg):
        rank = jnp.zeros(best.shape, jnp.int32)
        for i in range(epg):
            if i == j:
                continue
            ahead = (bv[i] > bv[j]) if i > j else (bv[i] >= bv[j])
            rank = rank + ahead.astype(jnp.int32)
        chosen.append(rank < 2)
    den = None
    for j in range(epg):
        term = jnp.where(chosen[j], sv[j], 0.0)
        den = term if den is None else den + term
    for g in range(N_GROUPS):
        for j in range(epg):
            e = g * epg + j
            hit = jnp.logical_and(chosen[j], best == g)
            ct_ref[e:e + 1, :] = jnp.where(hit, sv[j] / den, 0.0)
            sel_ref[e:e + 1, :] = jnp.where(hit, 1.0, 0.0)
    sel = sel_ref[...]
    tm = sel.shape[1]
    before = lax.broadcasted_iota(jnp.int32, (tm, tm), 0) < lax.broadcasted_iota(jnp.int32, (tm, tm), 1)
    carry = carry_ref[:, 0:1]
    rank = _dot(sel.astype(BF16), jnp.where(before, 1.0, 0.0).astype(BF16)) + carry
    rk_ref[...] = jnp.where(sel > 0.0, rank, -1.0).astype(jnp.int32)
    total = jnp.broadcast_to(carry + jnp.sum(sel, axis=1, keepdims=True), carry_ref.shape)
    carry_ref[...] = total
    cnt_ref[...] = total


def _router(xf, router_w, router_b, tm=1024):
    t = xf.shape[0]
    tm = min(tm, t)
    col = pl.BlockSpec((N_EXPERTS, tm), lambda i: (0, i))
    return pl.pallas_call(
        _router_kernel,
        out_shape=(jax.ShapeDtypeStruct((N_EXPERTS, t), F32), jax.ShapeDtypeStruct((N_EXPERTS, t), jnp.int32),
                   jax.ShapeDtypeStruct((N_EXPERTS, LANES), F32)),
        grid=(t // tm,),
        in_specs=[pl.BlockSpec((tm, D_MODEL), lambda i: (i, 0)),
                  pl.BlockSpec((D_MODEL, LANES), lambda i: (0, 0)),
                  pl.BlockSpec((N_EXPERTS, 1), lambda i: (0, 0))],
        out_specs=(col, col, pl.BlockSpec((N_EXPERTS, LANES), lambda i: (0, 0))),
        scratch_shapes=[pltpu.VMEM((N_EXPERTS, tm), F32), pltpu.VMEM((N_EXPERTS, LANES), F32)],
        compiler_params=_params(("arbitrary",)),
        name="router",
    )(xf, jnp.pad(router_w.astype(F32), ((0, 0), (0, LANES - N_EXPERTS))), router_b.astype(F32)[:, None])


def _sc_mesh():
    return plsc.VectorSubcoreMesh(core_axis_name="core", subcore_axis_name="subcore",
                                  num_cores=SC_CORES, num_subcores=SC_SUBCORES)


def _sc_scatter_rows(src, idx0, idx1, n_out):
    n, w = src.shape

    @pl.kernel(out_type=jax.ShapeDtypeStruct((n_out, w), src.dtype), mesh=_sc_mesh(), scratch_types=[],
               name="dispatch_rows")
    def run(x_hbm, i0_hbm, i1_hbm, o_hbm):
        def body(x_vmem, i0_vmem, i1_vmem):
            pltpu.sync_copy(x_vmem, o_hbm.at[i0_vmem.at[0]])
            pltpu.sync_copy(x_vmem, o_hbm.at[i1_vmem.at[0]])

        pltpu.emit_pipeline(
            body, grid=(n // SC_WINDOW,),
            in_specs=[pl.BlockSpec((SC_WINDOW, w), index_map=lambda i: (i, 0)),
                      pl.BlockSpec((1, SC_WINDOW), index_map=lambda i: (0, i)),
                      pl.BlockSpec((1, SC_WINDOW), index_map=lambda i: (0, i))],
            out_specs=[], core_axis_name=("core", "subcore"), dimension_semantics=(pltpu.PARALLEL,),
        )(x_hbm, i0_hbm, i1_hbm)

    return run(src, idx0.reshape(1, n), idx1.reshape(1, n))


def _sc_gather_rows(src, idx):
    n = idx.shape[0]
    w = src.shape[1]

    @pl.kernel(out_type=jax.ShapeDtypeStruct((n, w), src.dtype), mesh=_sc_mesh(), scratch_types=[],
               name="collect_rows")
    def run(x_hbm, i_hbm, o_hbm):
        def body(i_vmem, o_vmem):
            pltpu.sync_copy(x_hbm.at[i_vmem.at[0]], o_vmem)

        pltpu.emit_pipeline(
            body, grid=(n // SC_WINDOW,),
            in_specs=[pl.BlockSpec((1, SC_WINDOW), index_map=lambda i: (0, i))],
            out_specs=[pl.BlockSpec((SC_WINDOW, w), index_map=lambda i: (i, 0))],
            core_axis_name=("core", "subcore"), dimension_semantics=(pltpu.PARALLEL,),
        )(i_hbm, o_hbm)

    return run(src, idx.reshape(1, n))


def _dispatch_plan(rank_t, comb_t, counts):
    tg = EXPERT_TILE
    t = rank_t.shape[1]
    n_tiles = 2 * t // tg + N_EXPERTS
    cnt = counts[:, 0].astype(jnp.int32)
    tiles = (cnt + tg - 1) // tg
    tile_end = jnp.cumsum(tiles)
    seg_start = (tile_end - tiles) * tg
    n_active = tile_end[-1]
    tile_ids = jnp.arange(n_tiles, dtype=jnp.int32)
    tile_expert = jnp.sum(tile_end[None, :] <= jnp.minimum(tile_ids, n_active - 1)[:, None], axis=1).astype(jnp.int32)
    routed = rank_t >= 0
    pos = seg_start[:, None] + rank_t
    order = jnp.cumsum(routed.astype(jnp.int32), axis=0)
    first = jnp.logical_and(routed, order == 1)
    second = jnp.logical_and(routed, order == 2)
    dest0 = jnp.sum(jnp.where(first, pos, 0), axis=0)
    dest1 = jnp.sum(jnp.where(second, pos, 0), axis=0)
    w01 = jnp.stack([jnp.sum(jnp.where(first, comb_t, 0.0), axis=0), jnp.sum(jnp.where(second, comb_t, 0.0), axis=0)], axis=1)
    return dest0, dest1, w01, tile_expert, n_active.reshape(1).astype(jnp.int32), n_tiles * tg


def _expert_kernel(te_ref, na_ref, xa_ref, xb_ref, wg_ref, wu_ref, wd_ref, ya_ref, yb_ref, wgb_ref, wub_ref, wdb_ref):
    i = pl.program_id(0)
    live = i < na_ref[0]

    @pl.when(jnp.logical_or(i == 0, te_ref[i] != te_ref[jnp.maximum(i - 1, 0)]))
    def _():
        wgb_ref[...] = wg_ref[0, 0].astype(BF16)
        wub_ref[...] = wu_ref[0, 0].astype(BF16)
        wdb_ref[...] = wd_ref[0, 0].astype(BF16)

    @pl.when(live)
    def _():
        x = _unpack_row(xa_ref[...], xb_ref[...]).astype(BF16)
        h = jax.nn.silu(_dot(x, wgb_ref[...])) * _dot(x, wub_ref[...])
        ya_ref[...], yb_ref[...] = _pack_row(_dot(h.astype(BF16), wdb_ref[...]))

    @pl.when(jnp.logical_not(live))
    def _():
        ya_ref[...] = jnp.zeros_like(ya_ref)
        yb_ref[...] = jnp.zeros_like(yb_ref)


def _experts(xsa, xsb, wg, wu, wd, layer, tile_expert, n_active):
    n, q = xsa.shape
    tg = EXPERT_TILE
    rows = pl.BlockSpec((tg, q), lambda i, te, na: (i, 0))
    grid_spec = pltpu.PrefetchScalarGridSpec(
        num_scalar_prefetch=2, grid=(n // tg,),
        in_specs=[rows, rows,
                  pl.BlockSpec((1, 1, D_MODEL, D_EXPERT), lambda i, te, na: (layer, te[i], 0, 0)),
                  pl.BlockSpec((1, 1, D_MODEL, D_EXPERT), lambda i, te, na: (layer, te[i], 0, 0)),
                  pl.BlockSpec((1, 1, D_EXPERT, D_MODEL), lambda i, te, na: (layer, te[i], 0, 0))],
        out_specs=(rows, rows),
        scratch_shapes=[pltpu.VMEM((D_MODEL, D_EXPERT), BF16), pltpu.VMEM((D_MODEL, D_EXPERT), BF16),
                        pltpu.VMEM((D_EXPERT, D_MODEL), BF16)])
    shape = jax.ShapeDtypeStruct((n, q), jnp.int32)
    return pl.pallas_call(
        _expert_kernel, grid_spec=grid_spec, out_shape=(shape, shape),
        compiler_params=_params(("arbitrary",)), name="experts",
    )(tile_expert, n_active, xsa, xsb, wg, wu, wd)


def _combine_kernel(g0a_ref, g0b_ref, g1a_ref, g1b_ref, w_ref, x_ref, lg_ref, lb_ref, *rest):
    xo_ref, xbo_ref = rest[-2:]
    y0 = _unpack_row(g0a_ref[...], g0b_ref[...])
    y1 = _unpack_row(g1a_ref[...], g1b_ref[...])
    w = w_ref[...]
    xn = _layer_norm(ALPHA * x_ref[...] + (w[:, 0:1] * y0 + w[:, 1:2] * y1), lg_ref[...], lb_ref[...])
    xo_ref[...] = xn
    xbo_ref[...] = xn.astype(BF16)


def _combine(gathered, w01, xf, ln_g, ln_b, part, parts, prev=None, tm=1024):
    t = xf.shape[0]
    tp = t // parts
    tm = min(tm, tp)
    off = part * (tp // tm)
    tile = pl.BlockSpec((tm, D_MODEL), lambda i: (i + off, 0))
    packed = pl.BlockSpec((tm, D_MODEL // 4), lambda i: (i, 0))
    row = pl.BlockSpec((1, D_MODEL), lambda i: (0, 0))
    in_specs = [packed, packed, packed, packed, pl.BlockSpec((tm, 2), lambda i: (i + off, 0)), tile, row, row]
    args = list(gathered) + [w01, xf, ln_g.astype(F32)[None, :], ln_b.astype(F32)[None, :]]
    aliases = {}
    if prev is not None:
        aliases = {len(args): 0, len(args) + 1: 1}
        in_specs += [pl.BlockSpec(memory_space=pl.ANY), pl.BlockSpec(memory_space=pl.ANY)]
        args += list(prev)
    return pl.pallas_call(
        _combine_kernel,
        out_shape=(jax.ShapeDtypeStruct((t, D_MODEL), F32), jax.ShapeDtypeStruct((t, D_MODEL), BF16)),
        grid=(tp // tm,),
        in_specs=in_specs,
        out_specs=(tile, tile),
        input_output_aliases=aliases,
        compiler_params=_params(("arbitrary",)),
        name="combine_ln",
    )(*args)


def kernel(x, positions, w_in, gdn_conv_w, gdn_a_log, gdn_dt_bias, gdn_norm_g, hgrn_lb, hgrn_norm_g, merge_b,
           w_branch, w_out, ln1_g, ln1_b, router_w, router_b, moe_w_gate, moe_w_up, moe_w_down, ln2_g, ln2_b):
    b, s, d = x.shape
    t = b * s
    assert d == D_MODEL and w_in.shape[-1] == 2 * HALF_WIDTH + SMALL_WIDTH and SMALL_START == HALF_WIDTH

    w_lo = w_in[:, :, :HALF_WIDTH].astype(BF16)
    w_hi = w_in[:, :, SMALL_START + SMALL_WIDTH:].astype(BF16)
    w_small = jnp.pad(w_in[:, :, SMALL_START:SMALL_START + SMALL_WIDTH], ((0, 0), (0, 0), (0, LANES - SMALL_WIDTH))).astype(BF16)
    wb = w_branch.astype(BF16)
    wo = w_out.astype(BF16)
    wg, wu, wd = moe_w_gate, moe_w_up, moe_w_down

    cos, sin = _rope_tables(positions)
    cos3 = cos.reshape(b, s, HEAD_K)
    sin3 = sin.reshape(b, s, HEAD_K)

    xf = x.reshape(t, d).astype(F32)
    xb = xf.astype(BF16)
    for l in range(DEPTH):
        proj_lo, small = _in_proj(xb, w_lo, l, HALF_WIDTH, w_small=w_small)
        proj_hi = _in_proj(xb, w_hi, l, HALF_WIDTH)
        lo3 = proj_lo.reshape(b, s, HALF_WIDTH)
        hi3 = proj_hi.reshape(b, s, HALF_WIDTH)
        y_ret = _retention(lo3, cos3, sin3)
        y_gdn = _gdn(lo3, small.reshape(b, s, LANES), gdn_conv_w[l], gdn_a_log[l], gdn_dt_bias[l], gdn_norm_g[l])
        y_hg = _hgrn2(hi3, hgrn_lb, hgrn_norm_g[l], l)
        xf, pa, pb = _merge(y_ret.reshape(t, V_WIDTH), y_gdn.reshape(t, V_WIDTH), y_hg.reshape(t, V_WIDTH), proj_hi,
                            merge_b[l], wb, wo, l, xf, ln1_g[l], ln1_b[l])
        comb_t, rank_t, counts = _router(xf, router_w, router_b)
        dest0, dest1, w01, tile_expert, n_active, n_rows = _dispatch_plan(rank_t, comb_t, counts)
        ysa, ysb = _experts(_sc_scatter_rows(pa, dest0, dest1, n_rows), _sc_scatter_rows(pb, dest0, dest1, n_rows),
                            wg, wu, wd, l, tile_expert, n_active)
        tp = t // COMBINE_PARTS
        gathered = [[_sc_gather_rows(ys, dest[p * tp:(p + 1) * tp]) for dest in (dest0, dest1) for ys in (ysa, ysb)]
                    for p in range(COMBINE_PARTS)]
        out = None
        for p in range(COMBINE_PARTS):
            out = _combine(gathered[p], w01, xf, ln2_g[l], ln2_b[l], p, COMBINE_PARTS, prev=out)
        xf, xb = out
    return xf.reshape(b, s, d).astype(x.dtype)
```
